```python
import jax, jax.numpy as jnp
from jax import lax
import numpy as np

D_MODEL = 4096
BATCH = 2
SEQ = 4096
DEPTH = 2

EPS = 1e-6
N_MIXERS = 2
HEAD_DIM = 128
N_Q_HEADS = D_MODEL // HEAD_DIM
N_KV_HEADS = N_Q_HEADS // 4
GQA_GROUP = N_Q_HEADS // N_KV_HEADS
ROPE_THETA = 10000.0
AXIS_DIM = HEAD_DIM // 2
N_FREQ = AXIS_DIM // 2
GRID_W = 64
Q_BLOCK = 128
CHUNK = 128
GMLP_WIDTH = D_MODEL
GMLP_GROUPS = GMLP_WIDTH // 128
GMLP_GROUP_DIM = GMLP_WIDTH // GMLP_GROUPS
D_FF = 256 * ((8 * D_MODEL // 3 + 255) // 256)
N_EXPERTS = 8
TOP_K = 2
D_FF_EXPERT = D_MODEL
N_MOD = 6
MOD_SCALE = 0.3
N_ATTN = (DEPTH + 1) // 2
N_MIX = DEPTH // 2
N_DENSE = (DEPTH + 1) // 2
N_MOE = DEPTH // 2

kernel_name = "hybrid_attn_gmlp_moe_adaln_encoder"


def rmsnorm(x, g):
    xf = x.astype(jnp.float32)
    y = xf * lax.rsqrt(jnp.mean(xf * xf, axis=-1, keepdims=True) + EPS)
    return (y * g.astype(jnp.float32)).astype(x.dtype)


def modulate(x, g, shift, scale):
    return rmsnorm(x, g) * (1.0 + scale[:, None, :]) + shift[:, None, :]


def axial_rope_tables(seq):
    rows = seq // GRID_W
    row_pos = jnp.repeat(jnp.arange(rows, dtype=jnp.int32), GRID_W)
    col_pos = jnp.tile(jnp.arange(GRID_W, dtype=jnp.int32), rows)
    inv_freq = 1.0 / (ROPE_THETA ** (jnp.arange(N_FREQ, dtype=jnp.float32) * 2.0 / AXIS_DIM))
    ang = jnp.stack([row_pos.astype(jnp.float32)[:, None] * inv_freq,
                     col_pos.astype(jnp.float32)[:, None] * inv_freq], axis=1)
    return jnp.cos(ang), jnp.sin(ang)


def apply_axial_rope(x, cos, sin):
    B, S, H, _ = x.shape
    xr = x.astype(jnp.float32).reshape(B, S, H, 2, 2, N_FREQ)
    x1, x2 = xr[..., 0, :], xr[..., 1, :]
    c = cos[None, :, None]
    s = sin[None, :, None]
    out = jnp.stack([x1 * c - x2 * s, x2 * c + x1 * s], axis=-2)
    return out.reshape(B, S, H, HEAD_DIM).astype(x.dtype)


def attention_mixer(h, w_qkv, q_gain, k_gain, w_o, cos, sin):
    B, S, _ = h.shape
    qkv = h @ w_qkv
    q, k, v = jnp.split(qkv, [N_Q_HEADS * HEAD_DIM, (N_Q_HEADS + N_KV_HEADS) * HEAD_DIM], axis=-1)
    q = apply_axial_rope(rmsnorm(q.reshape(B, S, N_Q_HEADS, HEAD_DIM), q_gain), cos, sin)
    k = apply_axial_rope(rmsnorm(k.reshape(B, S, N_KV_HEADS, HEAD_DIM), k_gain), cos, sin)
    v = v.reshape(B, S, N_KV_HEADS, HEAD_DIM)
    qb = q.reshape(B, S // Q_BLOCK, Q_BLOCK, N_KV_HEADS, GQA_GROUP, HEAD_DIM)
    qb = jnp.moveaxis(qb, 1, 0)
    scale = HEAD_DIM ** -0.5

    def one_block(qi):
        s = jnp.einsum('bqkgd,bskd->bkgqs', qi, k,
                       preferred_element_type=jnp.float32) * scale
        p = jax.nn.softmax(s, axis=-1)
        return jnp.einsum('bkgqs,bskd->bqkgd', p.astype(v.dtype), v)

    o = lax.map(one_block, qb)
    o = jnp.moveaxis(o, 0, 1).reshape(B, S, N_Q_HEADS * HEAD_DIM)
    return o @ w_o


def spatial_gating_mixer(h, w_uv, v_gain, w_s, b_s, w_out):
    B, S, _ = h.shape
    z = jax.nn.gelu(h @ w_uv, approximate=False)
    u, v = jnp.split(z, 2, axis=-1)
    v = rmsnorm(v, v_gain)
    v = v.reshape(B, S // CHUNK, CHUNK, GMLP_GROUPS, GMLP_GROUP_DIM)
    sv = jnp.einsum('gpq,bnqgc->bnpgc', w_s, v) + b_s.T[None, None, :, :, None]
    return (u * sv.reshape(B, S, GMLP_WIDTH)) @ w_out


def swiglu(h, w_gate, w_up, w_down):
    return (jax.nn.silu(h @ w_gate) * (h @ w_up)) @ w_down


def moe_swiglu(h, w_router, w_gate, w_up, w_down):
    B, S, D = h.shape
    t = h.reshape(B * S, D)
    logits = (t @ w_router).astype(jnp.float32)
    top_val, top_idx = lax.top_k(logits, TOP_K)
    top_w = jax.nn.softmax(top_val, axis=-1)
    gates = jnp.sum(jax.nn.one_hot(top_idx, N_EXPERTS, dtype=jnp.float32) * top_w[..., None], axis=1)
    gates = gates.astype(t.dtype)
    out = jnp.zeros_like(t)
    for e in range(N_EXPERTS):
        y = (jax.nn.silu(t @ w_gate[e]) * (t @ w_up[e])) @ w_down[e]
        out = out + gates[:, e:e + 1] * y
    return out.reshape(B, S, D)


def setup_inputs(seed: int = 0) -> dict:
    key = jax.random.key(seed)
    ks = jax.random.split(key, 24)
    f32 = jnp.float32
    D = D_MODEL

    def nrm(k, shape, scale):
        return jax.random.normal(k, shape, f32) * scale

    def gain(k, shape):
        return 1.0 + 0.05 * jax.random.normal(k, shape, f32)

    qkv_out = (N_Q_HEADS + 2 * N_KV_HEADS) * HEAD_DIM
    return {
        "x": nrm(ks[0], (BATCH, SEQ, D), 1.0),
        "c": nrm(ks[1], (BATCH, D), 1.0),
        "w_mod": nrm(ks[2], (DEPTH, D, N_MOD * D), MOD_SCALE * D ** -0.5),
        "b_mod": nrm(ks[3], (DEPTH, N_MOD * D), 0.02),
        "norm_g": gain(ks[4], (DEPTH, 2, D)),
        "final_g": gain(ks[5], (D,)),
        "attn_w_qkv": nrm(ks[6], (N_ATTN, D, qkv_out), D ** -0.5),
        "attn_q_gain": gain(ks[7], (N_ATTN, HEAD_DIM)),
        "attn_k_gain": gain(ks[8], (N_ATTN, HEAD_DIM)),
        "attn_w_o": nrm(ks[9], (N_ATTN, N_Q_HEADS * HEAD_DIM, D), (N_Q_HEADS * HEAD_DIM) ** -0.5),
        "mix_w_uv": nrm(ks[10], (N_MIX, D, 2 * GMLP_WIDTH), D ** -0.5),
        "mix_v_gain": gain(ks[11], (N_MIX, GMLP_WIDTH)),
        "mix_w_s": nrm(ks[12], (N_MIX, GMLP_GROUPS, CHUNK, CHUNK), CHUNK ** -0.5),
        "mix_b_s": nrm(ks[13], (N_MIX, GMLP_GROUPS, CHUNK), 0.1),
        "mix_w_out": nrm(ks[14], (N_MIX, GMLP_WIDTH, D), GMLP_WIDTH ** -0.5),
        "ffn_w_gate": nrm(ks[15], (N_DENSE, D, D_FF), D ** -0.5),
        "ffn_w_up": nrm(ks[16], (N_DENSE, D, D_FF), D ** -0.5),
        "ffn_w_down": nrm(ks[17], (N_DENSE, D_FF, D), D_FF ** -0.5),
        "moe_w_router": nrm(ks[18], (N_MOE, D, N_EXPERTS), D ** -0.5),
        "moe_w_gate": nrm(ks[19], (N_MOE, N_EXPERTS, D, D_FF_EXPERT), D ** -0.5),
        "moe_w_up": nrm(ks[20], (N_MOE, N_EXPERTS, D, D_FF_EXPERT), D ** -0.5),
        "moe_w_down": nrm(ks[21], (N_MOE, N_EXPERTS, D_FF_EXPERT, D), D_FF_EXPERT ** -0.5),
    }


def reference(x, c, w_mod, b_mod, norm_g, final_g,
              attn_w_qkv, attn_q_gain, attn_k_gain, attn_w_o,
              mix_w_uv, mix_v_gain, mix_w_s, mix_b_s, mix_w_out,
              ffn_w_gate, ffn_w_up, ffn_w_down,
              moe_w_router, moe_w_gate, moe_w_up, moe_w_down):
    S = x.shape[1]
    cos, sin = axial_rope_tables(S)
    cond = jax.nn.silu(c)
    for i in range(DEPTH):
        mod = cond @ w_mod[i] + b_mod[i]
        sh1, sc1, g1, sh2, sc2, g2 = jnp.split(mod, N_MOD, axis=-1)
        j = i // N_MIXERS
        h = modulate(x, norm_g[i, 0], sh1, sc1)
        if i % N_MIXERS == 0:
            m = attention_mixer(h, attn_w_qkv[j], attn_q_gain[j], attn_k_gain[j],
                                attn_w_o[j], cos, sin)
        else:
            m = spatial_gating_mixer(h, mix_w_uv[j], mix_v_gain[j], mix_w_s[j],
                                     mix_b_s[j], mix_w_out[j])
        x = x + g1[:, None, :] * m
        h = modulate(x, norm_g[i, 1], sh2, sc2)
        if i % 2 == 0:
            f = swiglu(h, ffn_w_gate[i // 2], ffn_w_up[i // 2], ffn_w_down[i // 2])
        else:
            f = moe_swiglu(h, moe_w_router[i // 2], moe_w_gate[i // 2],
                           moe_w_up[i // 2], moe_w_down[i // 2])
        x = x + g2[:, None, :] * f
    return rmsnorm(x, final_g)
```

```python
import functools

import jax
import jax.numpy as jnp
from jax import lax
from jax.experimental import pallas as pl
from jax.experimental.pallas import tpu as pltpu

EPS = 1e-6
HEAD_DIM = 128
GQA_GROUP = 4
GRID_W = 64
ROPE_THETA = 10000.0
CHUNK = 128
TOP_K = 2
LANES = 128
SUBLANES = 8
VMEM_LIMIT_BYTES = 56 * 1024 * 1024

F32 = jnp.float32
BF16 = jnp.bfloat16


def _pick(dim, pref, mult):
    t = max(min(pref, dim) // mult * mult, mult)
    while t > mult and dim % t:
        t -= mult
    assert dim % t == 0, (dim, pref, mult)
    return t


def _params(*sem):
    return pltpu.CompilerParams(dimension_semantics=sem, vmem_limit_bytes=VMEM_LIMIT_BYTES)


def _silu(v):
    return v * jax.nn.sigmoid(v)


def _mod_kernel(c_ref, w_ref, b_ref, o_ref, acc_ref):
    k = pl.program_id(2)
    nb = c_ref.shape[0]
    tk, tn = w_ref.shape

    @pl.when(k == 0)
    def _():
        acc_ref[...] = jnp.zeros_like(acc_ref)

    for b in range(nb):
        cb = _silu(c_ref[b])
        parts = []
        for j in range(tn // LANES):
            prod = w_ref[:, j * LANES:(j + 1) * LANES] * cb
            parts.append(prod.reshape(tk // SUBLANES, SUBLANES, LANES).sum(axis=0))
        acc_ref[b] += jnp.concatenate(parts, axis=1)

    @pl.when(k == pl.num_programs(2) - 1)
    def _():
        for b in range(nb):
            o_ref[b:b + 1, :] = acc_ref[b].sum(axis=0, keepdims=True) + b_ref[...]


def _adaln_mod(c, w_mod, b_mod):
    depth, d, n = w_mod.shape
    nb = c.shape[0]
    tk = _pick(d, 512, SUBLANES)
    tn = _pick(n, 2048, LANES)
    c_rep = jnp.broadcast_to(c[:, :, None], (nb, d, LANES))
    return pl.pallas_call(
        _mod_kernel,
        grid=(depth, n // tn, d // tk),
        in_specs=[
            pl.BlockSpec((nb, tk, LANES), lambda l, j, k: (0, k, 0)),
            pl.BlockSpec((None, tk, tn), lambda l, j, k: (l, k, j)),
            pl.BlockSpec((None, 1, tn), lambda l, j, k: (l, 0, j)),
        ],
        out_specs=pl.BlockSpec((None, nb, tn), lambda l, j, k: (l, 0, j)),
        out_shape=jax.ShapeDtypeStruct((depth, nb, n), F32),
        scratch_shapes=[pltpu.VMEM((nb, SUBLANES, tn), F32)],
        compiler_params=_params("arbitrary", "arbitrary", "arbitrary"),
        name="adaln_mod",
    )(c_rep, w_mod, b_mod.reshape(depth, 1, n))


def _modulated(x_ref, g_ref, sh_ref, sc_ref):
    x = x_ref[...]
    y = x * lax.rsqrt(jnp.mean(x * x, axis=-1, keepdims=True) + EPS) * g_ref[...]
    return y * (1.0 + sc_ref[...]) + sh_ref[...]


def _modulate_kernel(x_ref, g_ref, sh_ref, sc_ref, o_ref):
    o_ref[...] = _modulated(x_ref, g_ref, sh_ref, sc_ref).astype(o_ref.dtype)


def _modulate_router_kernel(x_ref, g_ref, sh_ref, sc_ref, whi_ref, wlo_ref, o_ref, lg_ref):
    h = _modulated(x_ref, g_ref, sh_ref, sc_ref)
    hb = h.astype(BF16)
    o_ref[...] = hb
    h_lo = (h - hb.astype(F32)).astype(BF16)
    lg = jnp.dot(hb, whi_ref[...], preferred_element_type=F32)
    lg += jnp.dot(h_lo, whi_ref[...], preferred_element_type=F32)
    lg += jnp.dot(hb, wlo_ref[...], preferred_element_type=F32)
    lg_ref[...] = lg


def _row_specs(d, tm, tiles_per_batch):
    vec = pl.BlockSpec((None, 1, d), lambda i: (i // tiles_per_batch, 0, 0))
    return [
        pl.BlockSpec((tm, d), lambda i: (i, 0)),
        pl.BlockSpec((1, d), lambda i: (0, 0)),
        vec,
        vec,
    ]


def _modulate(x2, g, shift, scale, seq):
    m, d = x2.shape
    tm = _pick(seq, 256, SUBLANES)
    return pl.pallas_call(
        _modulate_kernel,
        grid=(m // tm,),
        in_specs=_row_specs(d, tm, seq // tm),
        out_specs=pl.BlockSpec((tm, d), lambda i: (i, 0)),
        out_shape=jax.ShapeDtypeStruct((m, d), BF16),
        compiler_params=_params("parallel"),
        name="modulate",
    )(x2, g.reshape(1, d), shift, scale)


def _modulate_router(x2, g, shift, scale, w_router, seq):
    m, d = x2.shape
    n_exp = w_router.shape[1]
    tm = _pick(seq, 256, SUBLANES)
    w_pad = jnp.zeros((d, LANES), F32).at[:, :n_exp].set(w_router)
    w_hi = w_pad.astype(BF16)
    w_lo = (w_pad - w_hi.astype(F32)).astype(BF16)
    wspec = pl.BlockSpec((d, LANES), lambda i: (0, 0))
    return pl.pallas_call(
        _modulate_router_kernel,
        grid=(m // tm,),
        in_specs=_row_specs(d, tm, seq // tm) + [wspec, wspec],
        out_specs=[pl.BlockSpec((tm, d), lambda i: (i, 0)),
                   pl.BlockSpec((tm, LANES), lambda i: (i, 0))],
        out_shape=[jax.ShapeDtypeStruct((m, d), BF16),
                   jax.ShapeDtypeStruct((m, LANES), F32)],
        compiler_params=_params("parallel"),
        name="modulate_router",
    )(x2, g.reshape(1, d), shift, scale, w_hi, w_lo)


def _cast_panel(w_ref, wb_ref):
    @pl.when(pl.program_id(1) == 0)
    def _():
        wb_ref[...] = w_ref[...].astype(BF16)


def _rope_rotate(v, first_half):
    return jnp.where(first_half, pltpu.roll(v, LANES - 32, 1), pltpu.roll(v, 32, 1))


def _qkv_kernel(a_ref, w_ref, gain_ref, cos_ref, sin_ref, o_ref, wb_ref, *, n_rope_tiles):
    _cast_panel(w_ref, wb_ref)
    acc = jnp.dot(a_ref[...], wb_ref[...], preferred_element_type=F32)
    j = pl.program_id(0)

    @pl.when(j < n_rope_tiles)
    def _():
        cos = cos_ref[...]
        sin = sin_ref[...]
        lane = lax.broadcasted_iota(jnp.int32, cos.shape, 1)
        first_half = (lane % 64) < 32
        for h in range(acc.shape[1] // HEAD_DIM):
            sl = slice(h * HEAD_DIM, (h + 1) * HEAD_DIM)
            v = acc[:, sl]
            v = v * lax.rsqrt(jnp.mean(v * v, axis=-1, keepdims=True) + EPS) * gain_ref[:, sl]
            o_ref[:, sl] = (v * cos + _rope_rotate(v, first_half) * sin).astype(o_ref.dtype)

    @pl.when(j >= n_rope_tiles)
    def _():
        o_ref[...] = acc.astype(o_ref.dtype)


def _qkv_proj(h, w_qkv, q_gain, k_gain, cos, sin, seq, n_q, n_kv):
    m, d = h.shape
    n = w_qkv.shape[1]
    tm = _pick(seq, 1024, SUBLANES)
    tn = _pick(n_kv * HEAD_DIM, 512, LANES)
    heads_per_tile = tn // HEAD_DIM
    n_q_tiles = n_q * HEAD_DIM // tn
    n_k_tiles = n_kv * HEAD_DIM // tn
    qg = jnp.tile(q_gain * HEAD_DIM ** -0.5, heads_per_tile)
    kg = jnp.tile(k_gain, heads_per_tile)
    gains = jnp.concatenate([jnp.broadcast_to(qg, (n_q_tiles, tn)),
                             jnp.broadcast_to(kg, (n_k_tiles, tn)),
                             jnp.ones((n_k_tiles, tn), F32)]).reshape(n // tn, 1, tn)
    tiles_per_seq = seq // tm
    tab = pl.BlockSpec((tm, HEAD_DIM), lambda j, i: (i % tiles_per_seq, 0))
    return pl.pallas_call(
        functools.partial(_qkv_kernel, n_rope_tiles=n_q_tiles + n_k_tiles),
        grid=(n // tn, m // tm),
        in_specs=[
            pl.BlockSpec((tm, d), lambda j, i: (i, 0)),
            pl.BlockSpec((d, tn), lambda j, i: (0, j)),
            pl.BlockSpec((None, 1, tn), lambda j, i: (j, 0, 0)),
            tab,
            tab,
        ],
        out_specs=pl.BlockSpec((tm, tn), lambda j, i: (i, j)),
        out_shape=jax.ShapeDtypeStruct((m, n), BF16),
        scratch_shapes=[pltpu.VMEM((d, tn), BF16)],
        compiler_params=_params("arbitrary", "arbitrary"),
        name="qkv_proj",
    )(h, w_qkv, gains, cos, sin)


def _gelu_kernel(a_ref, w_ref, o_ref, wb_ref):
    _cast_panel(w_ref, wb_ref)
    acc = jnp.dot(a_ref[...], wb_ref[...], preferred_element_type=F32)
    o_ref[...] = (0.5 * acc * (1.0 + lax.erf(acc * (2.0 ** -0.5)))).astype(o_ref.dtype)


def _gelu_proj(h, w):
    m, d = h.shape
    n = w.shape[1]
    tm = _pick(m, 1024, SUBLANES)
    tn = _pick(n, 512, LANES)
    return pl.pallas_call(
        _gelu_kernel,
        grid=(n // tn, m // tm),
        in_specs=[pl.BlockSpec((tm, d), lambda j, i: (i, 0)),
                  pl.BlockSpec((d, tn), lambda j, i: (0, j))],
        out_specs=pl.BlockSpec((tm, tn), lambda j, i: (i, j)),
        out_shape=jax.ShapeDtypeStruct((m, n), BF16),
        scratch_shapes=[pltpu.VMEM((d, tn), BF16)],
        compiler_params=_params("arbitrary", "arbitrary"),
        name="gelu_proj",
    )(h, w)


def _residual_kernel(a_ref, w_ref, x_ref, g_ref, o_ref, wb_ref):
    _cast_panel(w_ref, wb_ref)
    acc = jnp.dot(a_ref[...], wb_ref[...], preferred_element_type=F32)
    o_ref[...] = x_ref[...] + g_ref[...] * acc


def _residual_proj(a, w, x2, gate, seq, tm_pref, tn_pref, single_buffer_w):
    m, k = a.shape
    n = w.shape[1]
    tm = _pick(seq, tm_pref, SUBLANES)
    tn = _pick(n, tn_pref, LANES)
    tiles_per_batch = seq // tm
    w_mode = dict(pipeline_mode=pl.Buffered(1)) if single_buffer_w else {}
    return pl.pallas_call(
        _residual_kernel,
        grid=(n // tn, m // tm),
        in_specs=[
            pl.BlockSpec((tm, k), lambda j, i: (i, 0)),
            pl.BlockSpec((k, tn), lambda j, i: (0, j), **w_mode),
            pl.BlockSpec((tm, tn), lambda j, i: (i, j)),
            pl.BlockSpec((None, 1, tn), lambda j, i: (i // tiles_per_batch, 0, j)),
        ],
        out_specs=pl.BlockSpec((tm, tn), lambda j, i: (i, j)),
        out_shape=jax.ShapeDtypeStruct((m, n), F32),
        scratch_shapes=[pltpu.VMEM((k, tn), BF16)],
        compiler_params=_params("arbitrary", "arbitrary"),
        name="residual_proj",
    )(a, w, x2, gate)


def _swiglu_kernel(a_ref, wg_ref, wu_ref, o_ref, wgb_ref, wub_ref):
    _cast_panel(wg_ref, wgb_ref)
    _cast_panel(wu_ref, wub_ref)
    a = a_ref[...]
    gate = jnp.dot(a, wgb_ref[...], preferred_element_type=F32)
    up = jnp.dot(a, wub_ref[...], preferred_element_type=F32)
    o_ref[...] = (_silu(gate) * up).astype(o_ref.dtype)


def _swiglu_proj(h, w_gate, w_up):
    m, d = h.shape
    n = w_gate.shape[1]
    tm = _pick(m, 1024, SUBLANES)
    tn = _pick(n, 256, LANES)
    wspec = pl.BlockSpec((d, tn), lambda j, i: (0, j))
    return pl.pallas_call(
        _swiglu_kernel,
        grid=(n // tn, m // tm),
        in_specs=[pl.BlockSpec((tm, d), lambda j, i: (i, 0)), wspec, wspec],
        out_specs=pl.BlockSpec((tm, tn), lambda j, i: (i, j)),
        out_shape=jax.ShapeDtypeStruct((m, n), BF16),
        scratch_shapes=[pltpu.VMEM((d, tn), BF16), pltpu.VMEM((d, tn), BF16)],
        compiler_params=_params("arbitrary", "arbitrary"),
        name="swiglu_proj",
    )(h, w_gate, w_up)


def _attn_kernel(q_ref, k_ref, v_ref, o_ref, m_ref, l_ref, acc_ref, *, tk):
    tq = q_ref.shape[0]
    seq = k_ref.shape[0]
    q = jnp.concatenate(
        [q_ref[:, g * HEAD_DIM:(g + 1) * HEAD_DIM] for g in range(GQA_GROUP)], axis=0)
    m_ref[...] = jnp.full_like(m_ref, -jnp.inf)
    l_ref[...] = jnp.zeros_like(l_ref)
    acc_ref[...] = jnp.zeros_like(acc_ref)

    def body(c, carry):
        start = pl.multiple_of(c * tk, tk)
        k = k_ref[pl.ds(start, tk), :]
        v = v_ref[pl.ds(start, tk), :]
        s = lax.dot_general(q, k, (((1,), (1,)), ((), ())), preferred_element_type=F32)
        m_prev = m_ref[...]
        m_new = jnp.maximum(m_prev, jnp.max(s, axis=-1, keepdims=True))
        alpha = jnp.exp(m_prev - m_new)
        p = jnp.exp(s - m_new)
        l_ref[...] = alpha * l_ref[...] + jnp.sum(p, axis=-1, keepdims=True)
        acc_ref[...] = alpha * acc_ref[...] + jnp.dot(
            p.astype(BF16), v, preferred_element_type=F32)
        m_ref[...] = m_new
        return carry

    lax.fori_loop(0, seq // tk, body, 0)
    out = acc_ref[...] / l_ref[...]
    for g in range(GQA_GROUP):
        o_ref[:, g * HEAD_DIM:(g + 1) * HEAD_DIM] = out[g * tq:(g + 1) * tq].astype(o_ref.dtype)


def _attention(qkv, batch, seq, n_q, n_kv):
    m = qkv.shape[0]
    tq = _pick(seq, 256, 16)
    tk = _pick(seq, 512, 16)
    group_w = GQA_GROUP * HEAD_DIM
    q_tiles = seq // tq
    return pl.pallas_call(
        functools.partial(_attn_kernel, tk=tk),
        grid=(batch, n_kv, q_tiles),
        in_specs=[
            pl.BlockSpec((tq, group_w), lambda b, j, i: (b * q_tiles + i, j)),
            pl.BlockSpec((seq, HEAD_DIM), lambda b, j, i: (b, n_q + j)),
            pl.BlockSpec((seq, HEAD_DIM), lambda b, j, i: (b, n_q + n_kv + j)),
        ],
        out_specs=pl.BlockSpec((tq, group_w), lambda b, j, i: (b * q_tiles + i, j)),
        out_shape=jax.ShapeDtypeStruct((m, n_q * HEAD_DIM), BF16),
        scratch_shapes=[pltpu.VMEM((GQA_GROUP * tq, 1), F32),
                        pltpu.VMEM((GQA_GROUP * tq, 1), F32),
                        pltpu.VMEM((GQA_GROUP * tq, HEAD_DIM), F32)],
        compiler_params=_params("parallel", "parallel", "arbitrary"),
        name="attention",
    )(qkv, qkv, qkv)


def _sgu_kernel(u_ref, v_ref, gain_ref, ws_ref, bias_ref, o_ref, wsb_ref):
    @pl.when(pl.program_id(0) == 0)
    def _():
        wsb_ref[...] = ws_ref[...].astype(BF16)

    rows = u_ref.shape[0]
    groups = ws_ref.shape[0]
    for c in range(rows // CHUNK):
        rs = slice(c * CHUNK, (c + 1) * CHUNK)
        v = v_ref[rs, :].astype(F32)
        vn = v * lax.rsqrt(jnp.mean(v * v, axis=-1, keepdims=True) + EPS) * gain_ref[...]
        vn = vn.astype(BF16)
        for g in range(groups):
            cs = slice(g * LANES, (g + 1) * LANES)
            sv = jnp.dot(wsb_ref[g], vn[:, cs], preferred_element_type=F32) + bias_ref[g]
            o_ref[rs, cs] = (u_ref[rs, cs].astype(F32) * sv).astype(o_ref.dtype)


def _spatial_gate(z, v_gain, w_s, b_s):
    m, two_w = z.shape
    width = two_w // 2
    groups = w_s.shape[0]
    rows = _pick(m, 2 * CHUNK, CHUNK)
    bias = jnp.broadcast_to(b_s[:, :, None], (groups, CHUNK, LANES))
    full = lambda shape: pl.BlockSpec(shape, lambda i: (0,) * len(shape))
    return pl.pallas_call(
        _sgu_kernel,
        grid=(m // rows,),
        in_specs=[
            pl.BlockSpec((rows, width), lambda i: (i, 0)),
            pl.BlockSpec((rows, width), lambda i: (i, 1)),
            full((1, width)),
            full((groups, CHUNK, CHUNK)),
            full((groups, CHUNK, LANES)),
        ],
        out_specs=pl.BlockSpec((rows, width), lambda i: (i, 0)),
        out_shape=jax.ShapeDtypeStruct((m, width), BF16),
        scratch_shapes=[pltpu.VMEM((groups, CHUNK, CHUNK), BF16)],
        compiler_params=_params("arbitrary"),
        name="spatial_gate",
    )(z, z, v_gain.reshape(1, width), w_s, bias)


_I1, _I2, _W1, _W2, _R1, _R2 = range(6)


def _route_kernel(lg_ref, info_ref, cnt_ref, carry_ref, *, n_exp):
    @pl.when(pl.program_id(0) == 0)
    def _():
        carry_ref[...] = jnp.zeros_like(carry_ref)

    lg = lg_ref[...]
    tb = lg.shape[0]
    lane = lax.broadcasted_iota(jnp.int32, lg.shape, 1).astype(F32)
    lg = jnp.where(lane < n_exp, lg, -jnp.inf)
    v1 = jnp.max(lg, axis=-1, keepdims=True)
    i1 = jnp.min(jnp.where(lg == v1, lane, float(LANES)), axis=-1, keepdims=True)
    rest = jnp.where(lane == i1, -jnp.inf, lg)
    v2 = jnp.max(rest, axis=-1, keepdims=True)
    i2 = jnp.min(jnp.where(rest == v2, lane, float(LANES)), axis=-1, keepdims=True)
    e = jnp.exp(v2 - v1)
    w1 = 1.0 / (1.0 + e)
    w2 = e / (1.0 + e)
    sel = jnp.where(lane == i1, 1.0, jnp.where(lane == i2, 1.0, 0.0))
    row = lax.broadcasted_iota(jnp.int32, (tb, tb), 0)
    col = lax.broadcasted_iota(jnp.int32, (tb, tb), 1)
    tri = jnp.where(row > col, 1.0, 0.0).astype(BF16)
    rank = jnp.dot(tri, sel.astype(BF16), preferred_element_type=F32) + carry_ref[0:1, :]
    r1 = jnp.sum(jnp.where(lane == i1, rank, 0.0), axis=-1, keepdims=True)
    r2 = jnp.sum(jnp.where(lane == i2, rank, 0.0), axis=-1, keepdims=True)
    carry_ref[...] += jnp.sum(sel, axis=0, keepdims=True)
    info = jnp.zeros_like(lg)
    for idx, val in ((_I1, i1), (_I2, i2), (_W1, w1), (_W2, w2), (_R1, r1), (_R2, r2)):
        info = jnp.where(lane == idx, val, info)
    info_ref[...] = info
    cnt_ref[...] = carry_ref[...]


def _route(logits, n_exp):
    n = logits.shape[0]
    tb = _pick(n, 512, SUBLANES)
    return pl.pallas_call(
        functools.partial(_route_kernel, n_exp=n_exp),
        grid=(n // tb,),
        in_specs=[pl.BlockSpec((tb, LANES), lambda i: (i, 0))],
        out_specs=[pl.BlockSpec((tb, LANES), lambda i: (i, 0)),
                   pl.BlockSpec((SUBLANES, LANES), lambda i: (0, 0))],
        out_shape=[jax.ShapeDtypeStruct((n, LANES), F32),
                   jax.ShapeDtypeStruct((SUBLANES, LANES), F32)],
        scratch_shapes=[pltpu.VMEM((SUBLANES, LANES), F32)],
        compiler_params=_params("arbitrary"),
        name="route",
    )(logits)


def _row_copy_out(h_ref, xs_ref, sem, t, dst_row):
    return pltpu.make_async_copy(h_ref.at[t], xs_ref.at[dst_row], sem)


def _dispatch_kernel(pos_ref, h_ref, zero_ref, xs_ref, sem):
    del zero_ref
    tt = h_ref.shape[0]
    base = pl.program_id(0) * tt * TOP_K

    def issue(t, carry):
        for k in range(TOP_K):
            _row_copy_out(h_ref, xs_ref, sem, t, pos_ref[base + t * TOP_K + k]).start()
        return carry

    def drain(t, carry):
        for k in range(TOP_K):
            _row_copy_out(h_ref, xs_ref, sem, t, 0).wait()
        return carry

    lax.fori_loop(0, tt, issue, 0)
    lax.fori_loop(0, tt, drain, 0)


def _dispatch(h, pos, n_rows):
    n, d = h.shape
    sub = d // LANES
    tt = _pick(n, 256, SUBLANES)
    h3 = h.reshape(n, sub, LANES)
    zeros = jnp.zeros((n_rows, sub, LANES), h.dtype)
    xs = pl.pallas_call(
        _dispatch_kernel,
        grid_spec=pltpu.PrefetchScalarGridSpec(
            num_scalar_prefetch=1,
            grid=(n // tt,),
            in_specs=[pl.BlockSpec((tt, sub, LANES), lambda i, pos: (i, 0, 0)),
                      pl.BlockSpec(memory_space=pl.ANY)],
            out_specs=pl.BlockSpec(memory_space=pl.ANY),
            scratch_shapes=[pltpu.SemaphoreType.DMA(())],
        ),
        out_shape=jax.ShapeDtypeStruct((n_rows, sub, LANES), h.dtype),
        input_output_aliases={2: 0},
        compiler_params=_params("arbitrary"),
        name="dispatch",
    )(pos.reshape(-1), h3, zeros)
    return xs.reshape(n_rows, d)


def _tile_is_live(i, nt_ref):
    return i < nt_ref[0]


def _cast_expert_panel(w_ref, wb_ref, te_ref):
    i = pl.program_id(1)
    prev = te_ref[jnp.maximum(i - 1, 0)]

    @pl.when((i == 0) | (te_ref[i] != prev))
    def _():
        wb_ref[...] = w_ref[...].astype(BF16)


def _zero_dead_tile(o_ref, nt_ref):
    @pl.when(jnp.logical_not(_tile_is_live(pl.program_id(1), nt_ref)))
    def _():
        o_ref[...] = jnp.zeros_like(o_ref)


def _expert_swiglu_kernel(te_ref, nt_ref, a_ref, wg_ref, wu_ref, o_ref, wgb_ref, wub_ref):
    _zero_dead_tile(o_ref, nt_ref)

    @pl.when(_tile_is_live(pl.program_id(1), nt_ref))
    def _():
        _cast_expert_panel(wg_ref, wgb_ref, te_ref)
        _cast_expert_panel(wu_ref, wub_ref, te_ref)
        a = a_ref[...]
        gate = jnp.dot(a, wgb_ref[...], preferred_element_type=F32)
        up = jnp.dot(a, wub_ref[...], preferred_element_type=F32)
        o_ref[...] = (_silu(gate) * up).astype(o_ref.dtype)


def _expert_down_kernel(te_ref, nt_ref, a_ref, w_ref, o_ref, wb_ref):
    _zero_dead_tile(o_ref, nt_ref)

    @pl.when(_tile_is_live(pl.program_id(1), nt_ref))
    def _():
        _cast_expert_panel(w_ref, wb_ref, te_ref)
        o_ref[...] = jnp.dot(a_ref[...], wb_ref[...], preferred_element_type=F32)


def _live_tile(i, nt):
    return jnp.minimum(i, nt[0] - 1)


def _expert_swiglu(xs, w_gate, w_up, tile_expert, n_tiles, tm):
    rows, d = xs.shape
    n = w_gate.shape[2]
    tn = _pick(n, 256, LANES)
    wspec = pl.BlockSpec((None, d, tn), lambda j, i, te, nt: (te[_live_tile(i, nt)], 0, j))
    return pl.pallas_call(
        _expert_swiglu_kernel,
        grid_spec=pltpu.PrefetchScalarGridSpec(
            num_scalar_prefetch=2,
            grid=(n // tn, rows // tm),
            in_specs=[pl.BlockSpec((tm, d), lambda j, i, te, nt: (_live_tile(i, nt), 0)),
                      wspec, wspec],
            out_specs=pl.BlockSpec((tm, tn), lambda j, i, te, nt: (i, j)),
            scratch_shapes=[pltpu.VMEM((d, tn), BF16), pltpu.VMEM((d, tn), BF16)],
        ),
        out_shape=jax.ShapeDtypeStruct((rows, n), BF16),
        compiler_params=_params("arbitrary", "arbitrary"),
        name="expert_swiglu",
    )(tile_expert, n_tiles, xs, w_gate, w_up)


def _expert_down(act, w_down, tile_expert, n_tiles, tm):
    rows, k = act.shape
    n = w_down.shape[2]
    tn = _pick(n, 512, LANES)
    return pl.pallas_call(
        _expert_down_kernel,
        grid_spec=pltpu.PrefetchScalarGridSpec(
            num_scalar_prefetch=2,
            grid=(n // tn, rows // tm),
            in_specs=[pl.BlockSpec((tm, k), lambda j, i, te, nt: (_live_tile(i, nt), 0)),
                      pl.BlockSpec((None, k, tn),
                                   lambda j, i, te, nt: (te[_live_tile(i, nt)], 0, j))],
            out_specs=pl.BlockSpec((tm, tn), lambda j, i, te, nt: (i, j)),
            scratch_shapes=[pltpu.VMEM((k, tn), BF16)],
        ),
        out_shape=jax.ShapeDtypeStruct((rows, n), F32),
        compiler_params=_params("arbitrary", "arbitrary"),
        name="expert_down",
    )(tile_expert, n_tiles, act, w_down)


def _row_copy_in(y_ref, buf_ref, sem, k, t, src_row):
    return pltpu.make_async_copy(y_ref.at[pl.ds(src_row, 1), :],
                                 buf_ref.at[k, pl.ds(t, 1), :], sem)


def _combine_kernel(pos_ref, x_ref, g_ref, info_ref, fg_ref, y_ref, o_ref, buf_ref, sem):
    tt = x_ref.shape[0]
    base = pl.program_id(0) * tt * TOP_K

    def issue(t, carry):
        for k in range(TOP_K):
            _row_copy_in(y_ref, buf_ref, sem, k, t, pos_ref[base + t * TOP_K + k]).start()
        return carry

    def drain(t, carry):
        for k in range(TOP_K):
            _row_copy_in(y_ref, buf_ref, sem, k, t, 0).wait()
        return carry

    lax.fori_loop(0, tt, issue, 0)
    lax.fori_loop(0, tt, drain, 0)
    info = info_ref[...]
    f = info[:, _W1:_W1 + 1] * buf_ref[0] + info[:, _W2:_W2 + 1] * buf_ref[1]
    xn = x_ref[...] + g_ref[...] * f
    o_ref[...] = xn * lax.rsqrt(jnp.mean(xn * xn, axis=-1, keepdims=True) + EPS) * fg_ref[...]


def _combine_norm(x2, gate, info, pos, y, final_g, seq):
    n, d = x2.shape
    tt = _pick(seq, 128, SUBLANES)
    tiles_per_batch = seq // tt
    return pl.pallas_call(
        _combine_kernel,
        grid_spec=pltpu.PrefetchScalarGridSpec(
            num_scalar_prefetch=1,
            grid=(n // tt,),
            in_specs=[pl.BlockSpec((tt, d), lambda i, pos: (i, 0)),
                      pl.BlockSpec((None, 1, d), lambda i, pos: (i // tiles_per_batch, 0, 0)),
                      pl.BlockSpec((tt, LANES), lambda i, pos: (i, 0)),
                      pl.BlockSpec((1, d), lambda i, pos: (0, 0)),
                      pl.BlockSpec(memory_space=pl.ANY)],
            out_specs=pl.BlockSpec((tt, d), lambda i, pos: (i, 0)),
            scratch_shapes=[pltpu.VMEM((TOP_K, tt, d), F32), pltpu.SemaphoreType.DMA(())],
        ),
        out_shape=jax.ShapeDtypeStruct((n, d), F32),
        compiler_params=_params("arbitrary"),
        name="combine_norm",
    )(pos.reshape(-1), x2, gate, info, final_g.reshape(1, d), y)


def _moe_layer(x2, g, shift, scale, gate, w_router, w_gate, w_up, w_down, final_g, seq):
    n = x2.shape[0]
    n_exp = w_router.shape[1]
    h, logits = _modulate_router(x2, g, shift, scale, w_router, seq)
    info, cnt = _route(logits, n_exp)

    tm = _pick(n, 512, SUBLANES)
    max_tiles = n * TOP_K // tm + n_exp
    counts = cnt[0, :n_exp].astype(jnp.int32)
    tiles_per_exp = (counts + tm - 1) // tm
    tile_end = jnp.cumsum(tiles_per_exp)
    offsets = (tile_end - tiles_per_exp) * tm
    n_tiles = tile_end[-1:]
    tile_ids = jnp.arange(max_tiles, dtype=jnp.int32)
    tile_expert = jnp.minimum(
        jnp.sum((tile_ids[:, None] >= tile_end[None, :]).astype(jnp.int32), axis=1), n_exp - 1)
    idx = info[:, _I1:_I2 + 1].astype(jnp.int32)
    pos = offsets[idx] + info[:, _R1:_R2 + 1].astype(jnp.int32)

    xs = _dispatch(h, pos, max_tiles * tm)
    act = _expert_swiglu(xs, w_gate, w_up, tile_expert, n_tiles, tm)
    y = _expert_down(act, w_down, tile_expert, n_tiles, tm)
    return _combine_norm(x2, gate, info, pos, y, final_g, seq)


def _rope_tables(seq):
    n_freq = HEAD_DIM // 4
    t = jnp.arange(seq, dtype=jnp.int32)
    inv_freq = 1.0 / (ROPE_THETA ** (jnp.arange(n_freq, dtype=F32) * 2.0 / (HEAD_DIM // 2)))
    row = (t // GRID_W).astype(F32)[:, None] * inv_freq
    col = (t % GRID_W).astype(F32)[:, None] * inv_freq
    cos = jnp.concatenate([jnp.cos(row), jnp.cos(row), jnp.cos(col), jnp.cos(col)], axis=1)
    sin = jnp.concatenate([-jnp.sin(row), jnp.sin(row), -jnp.sin(col), jnp.sin(col)], axis=1)
    return cos, sin


def kernel(x, c, w_mod, b_mod, norm_g, final_g, attn_w_qkv, attn_q_gain, attn_k_gain, attn_w_o, mix_w_uv, mix_v_gain, mix_w_s, mix_b_s, mix_w_out, ffn_w_gate, ffn_w_up, ffn_w_down, moe_w_router, moe_w_gate, moe_w_up, moe_w_down):
    batch, seq, d = x.shape
    depth = w_mod.shape[0]
    assert depth == 2 and d % (GQA_GROUP * HEAD_DIM) == 0 and seq % CHUNK == 0
    n_q = d // HEAD_DIM
    n_kv = n_q // GQA_GROUP
    cos, sin = _rope_tables(seq)

    mod = _adaln_mod(c, w_mod, b_mod).reshape(depth, batch, 6, 1, d)
    vec = lambda layer, which: mod[layer, :, which]
    x2 = x.reshape(batch * seq, d)

    h = _modulate(x2, norm_g[0, 0], vec(0, 0), vec(0, 1), seq)
    qkv = _qkv_proj(h, attn_w_qkv[0], attn_q_gain[0], attn_k_gain[0], cos, sin, seq, n_q, n_kv)
    o = _attention(qkv, batch, seq, n_q, n_kv)
    x2 = _residual_proj(o, attn_w_o[0], x2, vec(0, 2), seq, 1024, 512, False)
    h = _modulate(x2, norm_g[0, 1], vec(0, 3), vec(0, 4), seq)
    act = _swiglu_proj(h, ffn_w_gate[0], ffn_w_up[0])
    x2 = _residual_proj(act, ffn_w_down[0], x2, vec(0, 5), seq, 256, 512, True)

    h = _modulate(x2, norm_g[1, 0], vec(1, 0), vec(1, 1), seq)
    z = _gelu_proj(h, mix_w_uv[0])
    gated = _spatial_gate(z, mix_v_gain[0], mix_w_s[0], mix_b_s[0])
    x2 = _residual_proj(gated, mix_w_out[0], x2, vec(1, 2), seq, 1024, 512, False)
    out = _moe_layer(x2, norm_g[1, 1], vec(1, 3), vec(1, 4), vec(1, 5), moe_w_router[0],
                     moe_w_gate[0], moe_w_up[0], moe_w_down[0], final_g, seq)
    return out.reshape(batch, seq, d)
```

```python
import functools

import jax
import jax.numpy as jnp
from jax import lax
from jax.experimental import pallas as pl
from jax.experimental.pallas import tpu as pltpu

EPS = 1e-6
HEAD_DIM = 128
GQA_GROUP = 4
GRID_W = 64
ROPE_THETA = 10000.0
CHUNK = 128
TOP_K = 2
LOG2_E = 1.4426950408889634
LANES = 128
SUBLANES = 8
VMEM_LIMIT_BYTES = 56 * 1024 * 1024

F32 = jnp.float32
BF16 = jnp.bfloat16


def _pick(dim, pref, mult):
    t = max(min(pref, dim) // mult * mult, mult)
    while t > mult and dim % t:
        t -= mult
    assert dim % t == 0, (dim, pref, mult)
    return t


def _params(*sem):
    return pltpu.CompilerParams(dimension_semantics=sem, vmem_limit_bytes=VMEM_LIMIT_BYTES)


def _silu(v):
    return v * jax.nn.sigmoid(v)


def _mod_kernel(c_ref, w_ref, b_ref, o_ref, acc_ref):
    k = pl.program_id(2)
    nb = c_ref.shape[0]
    tk, tn = w_ref.shape

    @pl.when(k == 0)
    def _():
        acc_ref[...] = jnp.zeros_like(acc_ref)

    for b in range(nb):
        cb = _silu(c_ref[b])
        parts = []
        for j in range(tn // LANES):
            prod = w_ref[:, j * LANES:(j + 1) * LANES] * cb
            parts.append(prod.reshape(tk // SUBLANES, SUBLANES, LANES).sum(axis=0))
        acc_ref[b] += jnp.concatenate(parts, axis=1)

    @pl.when(k == pl.num_programs(2) - 1)
    def _():
        for b in range(nb):
            o_ref[b:b + 1, :] = acc_ref[b].sum(axis=0, keepdims=True) + b_ref[...]


def _adaln_mod(c, w_mod, b_mod):
    depth, d, n = w_mod.shape
    nb = c.shape[0]
    tk = _pick(d, 512, SUBLANES)
    tn = _pick(n, 2048, LANES)
    c_rep = jnp.broadcast_to(c[:, :, None], (nb, d, LANES))
    return pl.pallas_call(
        _mod_kernel,
        grid=(depth, n // tn, d // tk),
        in_specs=[
            pl.BlockSpec((nb, tk, LANES), lambda l, j, k: (0, k, 0)),
            pl.BlockSpec((None, tk, tn), lambda l, j, k: (l, k, j)),
            pl.BlockSpec((None, 1, tn), lambda l, j, k: (l, 0, j)),
        ],
        out_specs=pl.BlockSpec((None, nb, tn), lambda l, j, k: (l, 0, j)),
        out_shape=jax.ShapeDtypeStruct((depth, nb, n), F32),
        scratch_shapes=[pltpu.VMEM((nb, SUBLANES, tn), F32)],
        compiler_params=_params("arbitrary", "arbitrary", "arbitrary"),
        name="adaln_mod",
    )(c_rep, w_mod, b_mod.reshape(depth, 1, n))


def _modulated(x_ref, g_ref, sh_ref, sc_ref):
    x = x_ref[...]
    y = x * lax.rsqrt(jnp.mean(x * x, axis=-1, keepdims=True) + EPS) * g_ref[...]
    return y * (1.0 + sc_ref[...]) + sh_ref[...]


def _modulate_kernel(x_ref, g_ref, sh_ref, sc_ref, o_ref):
    o_ref[...] = _modulated(x_ref, g_ref, sh_ref, sc_ref).astype(o_ref.dtype)


def _modulate_router_kernel(x_ref, g_ref, sh_ref, sc_ref, whi_ref, wlo_ref, o_ref, lg_ref):
    h = _modulated(x_ref, g_ref, sh_ref, sc_ref)
    hb = h.astype(BF16)
    o_ref[...] = hb
    h_lo = (h - hb.astype(F32)).astype(BF16)
    lg = jnp.dot(hb, whi_ref[...], preferred_element_type=F32)
    lg += jnp.dot(h_lo, whi_ref[...], preferred_element_type=F32)
    lg += jnp.dot(hb, wlo_ref[...], preferred_element_type=F32)
    lg_ref[...] = lg


def _row_specs(d, tm, tiles_per_batch):
    vec = pl.BlockSpec((None, 1, d), lambda i: (i // tiles_per_batch, 0, 0))
    return [
        pl.BlockSpec((tm, d), lambda i: (i, 0)),
        pl.BlockSpec((1, d), lambda i: (0, 0)),
        vec,
        vec,
    ]


def _modulate(x2, g, shift, scale, seq):
    m, d = x2.shape
    tm = _pick(seq, 256, SUBLANES)
    return pl.pallas_call(
        _modulate_kernel,
        grid=(m // tm,),
        in_specs=_row_specs(d, tm, seq // tm),
        out_specs=pl.BlockSpec((tm, d), lambda i: (i, 0)),
        out_shape=jax.ShapeDtypeStruct((m, d), BF16),
        compiler_params=_params("parallel"),
        name="modulate",
    )(x2, g.reshape(1, d), shift, scale)


def _modulate_router(x2, g, shift, scale, w_router, seq):
    m, d = x2.shape
    n_exp = w_router.shape[1]
    tm = _pick(seq, 256, SUBLANES)
    w_pad = jnp.zeros((d, LANES), F32).at[:, :n_exp].set(w_router)
    w_hi = w_pad.astype(BF16)
    w_lo = (w_pad - w_hi.astype(F32)).astype(BF16)
    wspec = pl.BlockSpec((d, LANES), lambda i: (0, 0))
    return pl.pallas_call(
        _modulate_router_kernel,
        grid=(m // tm,),
        in_specs=_row_specs(d, tm, seq // tm) + [wspec, wspec],
        out_specs=[pl.BlockSpec((tm, d), lambda i: (i, 0)),
                   pl.BlockSpec((tm, LANES), lambda i: (i, 0))],
        out_shape=[jax.ShapeDtypeStruct((m, d), BF16),
                   jax.ShapeDtypeStruct((m, LANES), F32)],
        compiler_params=_params("parallel"),
        name="modulate_router",
    )(x2, g.reshape(1, d), shift, scale, w_hi, w_lo)


def _cast_panel(w_ref, wb_ref):
    @pl.when(pl.program_id(1) == 0)
    def _():
        wb_ref[...] = w_ref[...].astype(BF16)


def _rope_rotate(v, first_half):
    return jnp.where(first_half, pltpu.roll(v, LANES - 32, 1), pltpu.roll(v, 32, 1))


def _qkv_kernel(a_ref, w_ref, gain_ref, cos_ref, sin_ref, o_ref, wb_ref, *, n_rope_tiles):
    _cast_panel(w_ref, wb_ref)
    acc = jnp.dot(a_ref[...], wb_ref[...], preferred_element_type=F32)
    j = pl.program_id(0)

    @pl.when(j < n_rope_tiles)
    def _():
        cos = cos_ref[...]
        sin = sin_ref[...]
        lane = lax.broadcasted_iota(jnp.int32, cos.shape, 1)
        first_half = (lane % 64) < 32
        for h in range(acc.shape[1] // HEAD_DIM):
            sl = slice(h * HEAD_DIM, (h + 1) * HEAD_DIM)
            v = acc[:, sl]
            v = v * lax.rsqrt(jnp.mean(v * v, axis=-1, keepdims=True) + EPS) * gain_ref[:, sl]
            o_ref[:, sl] = (v * cos + _rope_rotate(v, first_half) * sin).astype(o_ref.dtype)

    @pl.when(j >= n_rope_tiles)
    def _():
        o_ref[...] = acc.astype(o_ref.dtype)


def _qkv_proj(h, w_qkv, q_gain, k_gain, cos, sin, seq, n_q, n_kv):
    m, d = h.shape
    n = w_qkv.shape[1]
    tm = _pick(seq, 1024, SUBLANES)
    tn = _pick(n_kv * HEAD_DIM, 512, LANES)
    heads_per_tile = tn // HEAD_DIM
    n_q_tiles = n_q * HEAD_DIM // tn
    n_k_tiles = n_kv * HEAD_DIM // tn
    qg = jnp.tile(q_gain * (HEAD_DIM ** -0.5 * LOG2_E), heads_per_tile)
    kg = jnp.tile(k_gain, heads_per_tile)
    gains = jnp.concatenate([jnp.broadcast_to(qg, (n_q_tiles, tn)),
                             jnp.broadcast_to(kg, (n_k_tiles, tn)),
                             jnp.ones((n_k_tiles, tn), F32)]).reshape(n // tn, 1, tn)
    tiles_per_seq = seq // tm
    tab = pl.BlockSpec((tm, HEAD_DIM), lambda j, i: (i % tiles_per_seq, 0))
    return pl.pallas_call(
        functools.partial(_qkv_kernel, n_rope_tiles=n_q_tiles + n_k_tiles),
        grid=(n // tn, m // tm),
        in_specs=[
            pl.BlockSpec((tm, d), lambda j, i: (i, 0)),
            pl.BlockSpec((d, tn), lambda j, i: (0, j)),
            pl.BlockSpec((None, 1, tn), lambda j, i: (j, 0, 0)),
            tab,
            tab,
        ],
        out_specs=pl.BlockSpec((tm, tn), lambda j, i: (i, j)),
        out_shape=jax.ShapeDtypeStruct((m, n), BF16),
        scratch_shapes=[pltpu.VMEM((d, tn), BF16)],
        compiler_params=_params("arbitrary", "arbitrary"),
        name="qkv_proj",
    )(h, w_qkv, gains, cos, sin)


def _gelu_kernel(a_ref, w_ref, o_ref, wb_ref):
    _cast_panel(w_ref, wb_ref)
    acc = jnp.dot(a_ref[...], wb_ref[...], preferred_element_type=F32)
    o_ref[...] = (0.5 * acc * (1.0 + lax.erf(acc * (2.0 ** -0.5)))).astype(o_ref.dtype)


def _gelu_proj(h, w):
    m, d = h.shape
    n = w.shape[1]
    tm = _pick(m, 1024, SUBLANES)
    tn = _pick(n, 512, LANES)
    return pl.pallas_call(
        _gelu_kernel,
        grid=(n // tn, m // tm),
        in_specs=[pl.BlockSpec((tm, d), lambda j, i: (i, 0)),
                  pl.BlockSpec((d, tn), lambda j, i: (0, j))],
        out_specs=pl.BlockSpec((tm, tn), lambda j, i: (i, j)),
        out_shape=jax.ShapeDtypeStruct((m, n), BF16),
        scratch_shapes=[pltpu.VMEM((d, tn), BF16)],
        compiler_params=_params("arbitrary", "arbitrary"),
        name="gelu_proj",
    )(h, w)


def _residual_kernel(a_ref, w_ref, x_ref, g_ref, o_ref, wb_ref):
    _cast_panel(w_ref, wb_ref)
    acc = jnp.dot(a_ref[...], wb_ref[...], preferred_element_type=F32)
    o_ref[...] = x_ref[...] + g_ref[...] * acc


def _residual_proj(a, w, x2, gate, seq, tm_pref, tn_pref, single_buffer_w):
    m, k = a.shape
    n = w.shape[1]
    tm = _pick(seq, tm_pref, SUBLANES)
    tn = _pick(n, tn_pref, LANES)
    tiles_per_batch = seq // tm
    w_mode = dict(pipeline_mode=pl.Buffered(1)) if single_buffer_w else {}
    return pl.pallas_call(
        _residual_kernel,
        grid=(n // tn, m // tm),
        in_specs=[
            pl.BlockSpec((tm, k), lambda j, i: (i, 0)),
            pl.BlockSpec((k, tn), lambda j, i: (0, j), **w_mode),
            pl.BlockSpec((tm, tn), lambda j, i: (i, j)),
            pl.BlockSpec((None, 1, tn), lambda j, i: (i // tiles_per_batch, 0, j)),
        ],
        out_specs=pl.BlockSpec((tm, tn), lambda j, i: (i, j)),
        out_shape=jax.ShapeDtypeStruct((m, n), F32),
        scratch_shapes=[pltpu.VMEM((k, tn), BF16)],
        compiler_params=_params("arbitrary", "arbitrary"),
        name="residual_proj",
    )(a, w, x2, gate)


def _swiglu_kernel(a_ref, wg_ref, wu_ref, o_ref, wgb_ref, wub_ref):
    _cast_panel(wg_ref, wgb_ref)
    _cast_panel(wu_ref, wub_ref)
    a = a_ref[...]
    gate = jnp.dot(a, wgb_ref[...], preferred_element_type=F32)
    up = jnp.dot(a, wub_ref[...], preferred_element_type=F32)
    o_ref[...] = (_silu(gate) * up).astype(o_ref.dtype)


def _swiglu_proj(h, w_gate, w_up):
    m, d = h.shape
    n = w_gate.shape[1]
    tm = _pick(m, 1024, SUBLANES)
    tn = _pick(n, 256, LANES)
    wspec = pl.BlockSpec((d, tn), lambda j, i: (0, j))
    return pl.pallas_call(
        _swiglu_kernel,
        grid=(n // tn, m // tm),
        in_specs=[pl.BlockSpec((tm, d), lambda j, i: (i, 0)), wspec, wspec],
        out_specs=pl.BlockSpec((tm, tn), lambda j, i: (i, j)),
        out_shape=jax.ShapeDtypeStruct((m, n), BF16),
        scratch_shapes=[pltpu.VMEM((d, tn), BF16), pltpu.VMEM((d, tn), BF16)],
        compiler_params=_params("arbitrary", "arbitrary"),
        name="swiglu_proj",
    )(h, w_gate, w_up)


def _attn_kernel(q_ref, k_ref, v_ref, o_ref, m_ref, l_ref, acc_ref, *, tk):
    tq = q_ref.shape[0]
    seq = k_ref.shape[0]
    lane_tiles = tk // LANES
    q = jnp.concatenate(
        [q_ref[:, g * HEAD_DIM:(g + 1) * HEAD_DIM] for g in range(GQA_GROUP)], axis=0)
    m_ref[...] = jnp.full_like(m_ref, -jnp.inf)
    l_ref[...] = jnp.zeros_like(l_ref)
    acc_ref[...] = jnp.zeros_like(acc_ref)

    def body(c, carry):
        start = pl.multiple_of(c * tk, tk)
        k = k_ref[pl.ds(start, tk), :]
        v = v_ref[pl.ds(start, tk), :]
        s = lax.dot_general(q, k, (((1,), (1,)), ((), ())), preferred_element_type=F32)
        tiles = [s[:, t * LANES:(t + 1) * LANES] for t in range(lane_tiles)]
        tile_max = functools.reduce(jnp.maximum, tiles)
        m_prev = m_ref[...]
        m_new = jnp.maximum(m_prev, jnp.max(tile_max, axis=-1, keepdims=True))
        alpha = jnp.exp2(m_prev - m_new)
        p = [jnp.exp2(t - m_new) for t in tiles]
        l_ref[...] = alpha * l_ref[...] + functools.reduce(jnp.add, p)
        pb = jnp.concatenate([t.astype(BF16) for t in p], axis=1)
        acc_ref[...] = alpha * acc_ref[...] + jnp.dot(pb, v, preferred_element_type=F32)
        m_ref[...] = m_new
        return carry

    lax.fori_loop(0, seq // tk, body, 0, unroll=True)
    out = acc_ref[...] / jnp.sum(l_ref[...], axis=-1, keepdims=True)
    for g in range(GQA_GROUP):
        o_ref[:, g * HEAD_DIM:(g + 1) * HEAD_DIM] = out[g * tq:(g + 1) * tq].astype(o_ref.dtype)


def _attention(qkv, batch, seq, n_q, n_kv):
    m = qkv.shape[0]
    tq = _pick(seq, 128, 16)
    tk = _pick(seq, 512, LANES)
    group_w = GQA_GROUP * HEAD_DIM
    q_tiles = seq // tq
    return pl.pallas_call(
        functools.partial(_attn_kernel, tk=tk),
        grid=(batch, n_kv, q_tiles),
        in_specs=[
            pl.BlockSpec((tq, group_w), lambda b, j, i: (b * q_tiles + i, j)),
            pl.BlockSpec((seq, HEAD_DIM), lambda b, j, i: (b, n_q + j)),
            pl.BlockSpec((seq, HEAD_DIM), lambda b, j, i: (b, n_q + n_kv + j)),
        ],
        out_specs=pl.BlockSpec((tq, group_w), lambda b, j, i: (b * q_tiles + i, j)),
        out_shape=jax.ShapeDtypeStruct((m, n_q * HEAD_DIM), BF16),
        scratch_shapes=[pltpu.VMEM((GQA_GROUP * tq, LANES), F32),
                        pltpu.VMEM((GQA_GROUP * tq, LANES), F32),
                        pltpu.VMEM((GQA_GROUP * tq, HEAD_DIM), F32)],
        compiler_params=_params("parallel", "parallel", "arbitrary"),
        name="attention",
    )(qkv, qkv, qkv)


def _sgu_kernel(u_ref, v_ref, gain_ref, ws_ref, bias_ref, o_ref, wsb_ref):
    @pl.when(pl.program_id(0) == 0)
    def _():
        wsb_ref[...] = ws_ref[...].astype(BF16)

    rows = u_ref.shape[0]
    groups = ws_ref.shape[0]
    for c in range(rows // CHUNK):
        rs = slice(c * CHUNK, (c + 1) * CHUNK)
        v = v_ref[rs, :].astype(F32)
        vn = v * lax.rsqrt(jnp.mean(v * v, axis=-1, keepdims=True) + EPS) * gain_ref[...]
        vn = vn.astype(BF16)
        for g in range(groups):
            cs = slice(g * LANES, (g + 1) * LANES)
            sv = jnp.dot(wsb_ref[g], vn[:, cs], preferred_element_type=F32) + bias_ref[g]
            o_ref[rs, cs] = (u_ref[rs, cs].astype(F32) * sv).astype(o_ref.dtype)


def _spatial_gate(z, v_gain, w_s, b_s):
    m, two_w = z.shape
    width = two_w // 2
    groups = w_s.shape[0]
    rows = _pick(m, 2 * CHUNK, CHUNK)
    bias = jnp.broadcast_to(b_s[:, :, None], (groups, CHUNK, LANES))
    full = lambda shape: pl.BlockSpec(shape, lambda i: (0,) * len(shape))
    return pl.pallas_call(
        _sgu_kernel,
        grid=(m // rows,),
        in_specs=[
            pl.BlockSpec((rows, width), lambda i: (i, 0)),
            pl.BlockSpec((rows, width), lambda i: (i, 1)),
            full((1, width)),
            full((groups, CHUNK, CHUNK)),
            full((groups, CHUNK, LANES)),
        ],
        out_specs=pl.BlockSpec((rows, width), lambda i: (i, 0)),
        out_shape=jax.ShapeDtypeStruct((m, width), BF16),
        scratch_shapes=[pltpu.VMEM((groups, CHUNK, CHUNK), BF16)],
        compiler_params=_params("arbitrary"),
        name="spatial_gate",
    )(z, z, v_gain.reshape(1, width), w_s, bias)


_I1, _I2, _W1, _W2, _R1, _R2 = range(6)


def _route_kernel(lg_ref, info_ref, cnt_ref, carry_ref, *, n_exp):
    @pl.when(pl.program_id(0) == 0)
    def _():
        carry_ref[...] = jnp.zeros_like(carry_ref)

    lg = lg_ref[...]
    tb = lg.shape[0]
    lane = lax.broadcasted_iota(jnp.int32, lg.shape, 1).astype(F32)
    lg = jnp.where(lane < n_exp, lg, -jnp.inf)
    v1 = jnp.max(lg, axis=-1, keepdims=True)
    i1 = jnp.min(jnp.where(lg == v1, lane, float(LANES)), axis=-1, keepdims=True)
    rest = jnp.where(lane == i1, -jnp.inf, lg)
    v2 = jnp.max(rest, axis=-1, keepdims=True)
    i2 = jnp.min(jnp.where(rest == v2, lane, float(LANES)), axis=-1, keepdims=True)
    e = jnp.exp(v2 - v1)
    w1 = 1.0 / (1.0 + e)
    w2 = e / (1.0 + e)
    sel = jnp.where(lane == i1, 1.0, jnp.where(lane == i2, 1.0, 0.0))
    row = lax.broadcasted_iota(jnp.int32, (tb, tb), 0)
    col = lax.broadcasted_iota(jnp.int32, (tb, tb), 1)
    tri = jnp.where(row > col, 1.0, 0.0).astype(BF16)
    rank = jnp.dot(tri, sel.astype(BF16), preferred_element_type=F32) + carry_ref[0:1, :]
    r1 = jnp.sum(jnp.where(lane == i1, rank, 0.0), axis=-1, keepdims=True)
    r2 = jnp.sum(jnp.where(lane == i2, rank, 0.0), axis=-1, keepdims=True)
    carry_ref[...] += jnp.sum(sel, axis=0, keepdims=True)
    info = jnp.zeros_like(lg)
    for idx, val in ((_I1, i1), (_I2, i2), (_W1, w1), (_W2, w2), (_R1, r1), (_R2, r2)):
        info = jnp.where(lane == idx, val, info)
    info_ref[...] = info
    cnt_ref[...] = carry_ref[...]


def _route(logits, n_exp):
    n = logits.shape[0]
    tb = _pick(n, 512, SUBLANES)
    return pl.pallas_call(
        functools.partial(_route_kernel, n_exp=n_exp),
        grid=(n // tb,),
        in_specs=[pl.BlockSpec((tb, LANES), lambda i: (i, 0))],
        out_specs=[pl.BlockSpec((tb, LANES), lambda i: (i, 0)),
                   pl.BlockSpec((SUBLANES, LANES), lambda i: (0, 0))],
        out_shape=[jax.ShapeDtypeStruct((n, LANES), F32),
                   jax.ShapeDtypeStruct((SUBLANES, LANES), F32)],
        scratch_shapes=[pltpu.VMEM((SUBLANES, LANES), F32)],
        compiler_params=_params("arbitrary"),
        name="route",
    )(logits)


def _row_copy_out(h_ref, xs_ref, sem, t, dst_row):
    return pltpu.make_async_copy(h_ref.at[t], xs_ref.at[dst_row], sem)


def _dispatch_kernel(pos_ref, h_ref, zero_ref, xs_ref, sem):
    del zero_ref
    tt = h_ref.shape[0]
    base = pl.program_id(0) * tt * TOP_K

    def issue(t, carry):
        for k in range(TOP_K):
            _row_copy_out(h_ref, xs_ref, sem, t, pos_ref[base + t * TOP_K + k]).start()
        return carry

    def drain(t, carry):
        for k in range(TOP_K):
            _row_copy_out(h_ref, xs_ref, sem, t, 0).wait()
        return carry

    lax.fori_loop(0, tt, issue, 0)
    lax.fori_loop(0, tt, drain, 0)


def _dispatch(h, pos, n_rows):
    n, d = h.shape
    sub = d // LANES
    tt = _pick(n, 256, SUBLANES)
    h3 = h.reshape(n, sub, LANES)
    zeros = jnp.zeros((n_rows, sub, LANES), h.dtype)
    xs = pl.pallas_call(
        _dispatch_kernel,
        grid_spec=pltpu.PrefetchScalarGridSpec(
            num_scalar_prefetch=1,
            grid=(n // tt,),
            in_specs=[pl.BlockSpec((tt, sub, LANES), lambda i, pos: (i, 0, 0)),
                      pl.BlockSpec(memory_space=pl.ANY)],
            out_specs=pl.BlockSpec(memory_space=pl.ANY),
            scratch_shapes=[pltpu.SemaphoreType.DMA(())],
        ),
        out_shape=jax.ShapeDtypeStruct((n_rows, sub, LANES), h.dtype),
        input_output_aliases={2: 0},
        compiler_params=_params("arbitrary"),
        name="dispatch",
    )(pos.reshape(-1), h3, zeros)
    return xs.reshape(n_rows, d)


def _tile_is_live(i, nt_ref):
    return i < nt_ref[0]


def _cast_expert_panel(w_ref, wb_ref, te_ref):
    i = pl.program_id(1)
    prev = te_ref[jnp.maximum(i - 1, 0)]

    @pl.when((i == 0) | (te_ref[i] != prev))
    def _():
        wb_ref[...] = w_ref[...].astype(BF16)


def _zero_dead_tile(o_ref, nt_ref):
    @pl.when(jnp.logical_not(_tile_is_live(pl.program_id(1), nt_ref)))
    def _():
        o_ref[...] = jnp.zeros_like(o_ref)


def _expert_swiglu_kernel(te_ref, nt_ref, a_ref, wg_ref, wu_ref, o_ref, wgb_ref, wub_ref):
    _zero_dead_tile(o_ref, nt_ref)

    @pl.when(_tile_is_live(pl.program_id(1), nt_ref))
    def _():
        _cast_expert_panel(wg_ref, wgb_ref, te_ref)
        _cast_expert_panel(wu_ref, wub_ref, te_ref)
        a = a_ref[...]
        gate = jnp.dot(a, wgb_ref[...], preferred_element_type=F32)
        up = jnp.dot(a, wub_ref[...], preferred_element_type=F32)
        o_ref[...] = (_silu(gate) * up).astype(o_ref.dtype)


def _expert_down_kernel(te_ref, nt_ref, a_ref, w_ref, o_ref, wb_ref):
    _zero_dead_tile(o_ref, nt_ref)

    @pl.when(_tile_is_live(pl.program_id(1), nt_ref))
    def _():
        _cast_expert_panel(w_ref, wb_ref, te_ref)
        o_ref[...] = jnp.dot(a_ref[...], wb_ref[...], preferred_element_type=F32)


def _live_tile(i, nt):
    return jnp.minimum(i, nt[0] - 1)


def _expert_swiglu(xs, w_gate, w_up, tile_expert, n_tiles, tm):
    rows, d = xs.shape
    n = w_gate.shape[2]
    tn = _pick(n, 512, LANES)
    wspec = pl.BlockSpec((None, d, tn), lambda j, i, te, nt: (te[_live_tile(i, nt)], 0, j))
    return pl.pallas_call(
        _expert_swiglu_kernel,
        grid_spec=pltpu.PrefetchScalarGridSpec(
            num_scalar_prefetch=2,
            grid=(n // tn, rows // tm),
            in_specs=[pl.BlockSpec((tm, d), lambda j, i, te, nt: (_live_tile(i, nt), 0)),
                      wspec, wspec],
            out_specs=pl.BlockSpec((tm, tn), lambda j, i, te, nt: (i, j)),
            scratch_shapes=[pltpu.VMEM((d, tn), BF16), pltpu.VMEM((d, tn), BF16)],
        ),
        out_shape=jax.ShapeDtypeStruct((rows, n), BF16),
        compiler_params=_params("arbitrary", "arbitrary"),
        name="expert_swiglu",
    )(tile_expert, n_tiles, xs, w_gate, w_up)


def _expert_down(act, w_down, tile_expert, n_tiles, tm):
    rows, k = act.shape
    n = w_down.shape[2]
    tn = _pick(n, 512, LANES)
    return pl.pallas_call(
        _expert_down_kernel,
        grid_spec=pltpu.PrefetchScalarGridSpec(
            num_scalar_prefetch=2,
            grid=(n // tn, rows // tm),
            in_specs=[pl.BlockSpec((tm, k), lambda j, i, te, nt: (_live_tile(i, nt), 0)),
                      pl.BlockSpec((None, k, tn),
                                   lambda j, i, te, nt: (te[_live_tile(i, nt)], 0, j))],
            out_specs=pl.BlockSpec((tm, tn), lambda j, i, te, nt: (i, j)),
            scratch_shapes=[pltpu.VMEM((k, tn), BF16)],
        ),
        out_shape=jax.ShapeDtypeStruct((rows, n), F32),
        compiler_params=_params("arbitrary", "arbitrary"),
        name="expert_down",
    )(tile_expert, n_tiles, act, w_down)


def _row_copy_in(y_ref, buf_ref, sem, k, t, src_row):
    return pltpu.make_async_copy(y_ref.at[pl.ds(src_row, 1), :],
                                 buf_ref.at[k, pl.ds(t, 1), :], sem)


def _combine_kernel(pos_ref, x_ref, g_ref, info_ref, fg_ref, y_ref, o_ref, buf_ref, sem):
    tt = x_ref.shape[0]
    base = pl.program_id(0) * tt * TOP_K

    def issue(t, carry):
        for k in range(TOP_K):
            _row_copy_in(y_ref, buf_ref, sem, k, t, pos_ref[base + t * TOP_K + k]).start()
        return carry

    def drain(t, carry):
        for k in range(TOP_K):
            _row_copy_in(y_ref, buf_ref, sem, k, t, 0).wait()
        return carry

    lax.fori_loop(0, tt, issue, 0)
    lax.fori_loop(0, tt, drain, 0)
    info = info_ref[...]
    f = info[:, _W1:_W1 + 1] * buf_ref[0] + info[:, _W2:_W2 + 1] * buf_ref[1]
    xn = x_ref[...] + g_ref[...] * f
    o_ref[...] = xn * lax.rsqrt(jnp.mean(xn * xn, axis=-1, keepdims=True) + EPS) * fg_ref[...]


def _combine_norm(x2, gate, info, pos, y, final_g, seq):
    n, d = x2.shape
    tt = _pick(seq, 128, SUBLANES)
    tiles_per_batch = seq // tt
    return pl.pallas_call(
        _combine_kernel,
        grid_spec=pltpu.PrefetchScalarGridSpec(
            num_scalar_prefetch=1,
            grid=(n // tt,),
            in_specs=[pl.BlockSpec((tt, d), lambda i, pos: (i, 0)),
                      pl.BlockSpec((None, 1, d), lambda i, pos: (i // tiles_per_batch, 0, 0)),
                      pl.BlockSpec((tt, LANES), lambda i, pos: (i, 0)),
                      pl.BlockSpec((1, d), lambda i, pos: (0, 0)),
                      pl.BlockSpec(memory_space=pl.ANY)],
            out_specs=pl.BlockSpec((tt, d), lambda i, pos: (i, 0)),
            scratch_shapes=[pltpu.VMEM((TOP_K, tt, d), F32), pltpu.SemaphoreType.DMA(())],
        ),
        out_shape=jax.ShapeDtypeStruct((n, d), F32),
        compiler_params=_params("arbitrary"),
        name="combine_norm",
    )(pos.reshape(-1), x2, gate, info, final_g.reshape(1, d), y)


def _moe_layer(x2, g, shift, scale, gate, w_router, w_gate, w_up, w_down, final_g, seq):
    n = x2.shape[0]
    n_exp = w_router.shape[1]
    h, logits = _modulate_router(x2, g, shift, scale, w_router, seq)
    info, cnt = _route(logits, n_exp)

    tm = _pick(n, 512, SUBLANES)
    max_tiles = n * TOP_K // tm + n_exp
    counts = cnt[0, :n_exp].astype(jnp.int32)
    tiles_per_exp = (counts + tm - 1) // tm
    tile_end = jnp.cumsum(tiles_per_exp)
    offsets = (tile_end - tiles_per_exp) * tm
    n_tiles = tile_end[-1:]
    tile_ids = jnp.arange(max_tiles, dtype=jnp.int32)
    tile_expert = jnp.minimum(
        jnp.sum((tile_ids[:, None] >= tile_end[None, :]).astype(jnp.int32), axis=1), n_exp - 1)
    idx = info[:, _I1:_I2 + 1].astype(jnp.int32)
    pos = offsets[idx] + info[:, _R1:_R2 + 1].astype(jnp.int32)

    xs = _dispatch(h, pos, max_tiles * tm)
    act = _expert_swiglu(xs, w_gate, w_up, tile_expert, n_tiles, tm)
    y = _expert_down(act, w_down, tile_expert, n_tiles, tm)
    return _combine_norm(x2, gate, info, pos, y, final_g, seq)


def _rope_tables(seq):
    n_freq = HEAD_DIM // 4
    t = jnp.arange(seq, dtype=jnp.int32)
    inv_freq = 1.0 / (ROPE_THETA ** (jnp.arange(n_freq, dtype=F32) * 2.0 / (HEAD_DIM // 2)))
    row = (t // GRID_W).astype(F32)[:, None] * inv_freq
    col = (t % GRID_W).astype(F32)[:, None] * inv_freq
    cos = jnp.concatenate([jnp.cos(row), jnp.cos(row), jnp.cos(col), jnp.cos(col)], axis=1)
    sin = jnp.concatenate([-jnp.sin(row), jnp.sin(row), -jnp.sin(col), jnp.sin(col)], axis=1)
    return cos, sin


def kernel(x, c, w_mod, b_mod, norm_g, final_g, attn_w_qkv, attn_q_gain, attn_k_gain, attn_w_o, mix_w_uv, mix_v_gain, mix_w_s, mix_b_s, mix_w_out, ffn_w_gate, ffn_w_up, ffn_w_down, moe_w_router, moe_w_gate, moe_w_up, moe_w_down):
    batch, seq, d = x.shape
    depth = w_mod.shape[0]
    assert depth == 2 and d % (GQA_GROUP * HEAD_DIM) == 0 and seq % CHUNK == 0
    n_q = d // HEAD_DIM
    n_kv = n_q // GQA_GROUP
    cos, sin = _rope_tables(seq)

    mod = _adaln_mod(c, w_mod, b_mod).reshape(depth, batch, 6, 1, d)
    vec = lambda layer, which: mod[layer, :, which]
    x2 = x.reshape(batch * seq, d)

    h = _modulate(x2, norm_g[0, 0], vec(0, 0), vec(0, 1), seq)
    qkv = _qkv_proj(h, attn_w_qkv[0], attn_q_gain[0], attn_k_gain[0], cos, sin, seq, n_q, n_kv)
    o = _attention(qkv, batch, seq, n_q, n_kv)
    x2 = _residual_proj(o, attn_w_o[0], x2, vec(0, 2), seq, 1024, 512, False)
    h = _modulate(x2, norm_g[0, 1], vec(0, 3), vec(0, 4), seq)
    act = _swiglu_proj(h, ffn_w_gate[0], ffn_w_up[0])
    x2 = _residual_proj(act, ffn_w_down[0], x2, vec(0, 5), seq, 256, 512, True)

    h = _modulate(x2, norm_g[1, 0], vec(1, 0), vec(1, 1), seq)
    z = _gelu_proj(h, mix_w_uv[0])
    gated = _spatial_gate(z, mix_v_gain[0], mix_w_s[0], mix_b_s[0])
    x2 = _residual_proj(gated, mix_w_out[0], x2, vec(1, 2), seq, 1024, 512, False)
    out = _moe_layer(x2, norm_g[1, 1], vec(1, 3), vec(1, 4), vec(1, 5), moe_w_router[0],
                     moe_w_gate[0], moe_w_up[0], moe_w_down[0], final_g, seq)
    return out.reshape(batch, seq, d)
```

```python
import functools

import jax
import jax.numpy as jnp
from jax import lax
from jax.experimental import pallas as pl
from jax.experimental.pallas import tpu as pltpu

EPS = 1e-6
HEAD_DIM = 128
GQA_GROUP = 4
GRID_W = 64
ROPE_THETA = 10000.0
CHUNK = 128
TOP_K = 2
LOG2_E = 1.4426950408889634
LANES = 128
SUBLANES = 8
QKV_ROW_SPLITS = 4
GELU_ROW_SPLITS = 4
VMEM_LIMIT_BYTES = 56 * 1024 * 1024

F32 = jnp.float32
BF16 = jnp.bfloat16


def _pick(dim, pref, mult):
    t = max(min(pref, dim) // mult * mult, mult)
    while t > mult and dim % t:
        t -= mult
    assert dim % t == 0, (dim, pref, mult)
    return t


def _params(*sem):
    return pltpu.CompilerParams(dimension_semantics=sem, vmem_limit_bytes=VMEM_LIMIT_BYTES)


def _silu(v):
    return v * jax.nn.sigmoid(v)


def _mod_kernel(c_ref, w_ref, b_ref, o_ref, acc_ref):
    k = pl.program_id(2)
    nb = c_ref.shape[0]
    tk, tn = w_ref.shape

    @pl.when(k == 0)
    def _():
        acc_ref[...] = jnp.zeros_like(acc_ref)

    for b in range(nb):
        cb = _silu(c_ref[b])
        parts = []
        for j in range(tn // LANES):
            prod = w_ref[:, j * LANES:(j + 1) * LANES] * cb
            parts.append(prod.reshape(tk // SUBLANES, SUBLANES, LANES).sum(axis=0))
        acc_ref[b] += jnp.concatenate(parts, axis=1)

    @pl.when(k == pl.num_programs(2) - 1)
    def _():
        for b in range(nb):
            o_ref[b:b + 1, :] = acc_ref[b].sum(axis=0, keepdims=True) + b_ref[...]


def _adaln_mod(c, w_mod, b_mod):
    depth, d, n = w_mod.shape
    nb = c.shape[0]
    tk = _pick(d, 512, SUBLANES)
    tn = _pick(n, 2048, LANES)
    c_rep = jnp.broadcast_to(c[:, :, None], (nb, d, LANES))
    return pl.pallas_call(
        _mod_kernel,
        grid=(depth, n // tn, d // tk),
        in_specs=[
            pl.BlockSpec((nb, tk, LANES), lambda l, j, k: (0, k, 0)),
            pl.BlockSpec((None, tk, tn), lambda l, j, k: (l, k, j)),
            pl.BlockSpec((None, 1, tn), lambda l, j, k: (l, 0, j)),
        ],
        out_specs=pl.BlockSpec((None, nb, tn), lambda l, j, k: (l, 0, j)),
        out_shape=jax.ShapeDtypeStruct((depth, nb, n), F32),
        scratch_shapes=[pltpu.VMEM((nb, SUBLANES, tn), F32)],
        compiler_params=_params("arbitrary", "arbitrary", "arbitrary"),
        name="adaln_mod",
    )(c_rep, w_mod, b_mod.reshape(depth, 1, n))


def _modulated(x_ref, g_ref, sh_ref, sc_ref):
    x = x_ref[...]
    y = x * lax.rsqrt(jnp.mean(x * x, axis=-1, keepdims=True) + EPS) * g_ref[...]
    return y * (1.0 + sc_ref[...]) + sh_ref[...]


def _modulate_kernel(x_ref, g_ref, sh_ref, sc_ref, o_ref):
    o_ref[...] = _modulated(x_ref, g_ref, sh_ref, sc_ref).astype(o_ref.dtype)


def _modulate_router_kernel(x_ref, g_ref, sh_ref, sc_ref, whi_ref, wlo_ref, o_ref, lg_ref):
    h = _modulated(x_ref, g_ref, sh_ref, sc_ref)
    hb = h.astype(BF16)
    o_ref[...] = hb
    h_lo = (h - hb.astype(F32)).astype(BF16)
    lg = jnp.dot(hb, whi_ref[...], preferred_element_type=F32)
    lg += jnp.dot(h_lo, whi_ref[...], preferred_element_type=F32)
    lg += jnp.dot(hb, wlo_ref[...], preferred_element_type=F32)
    lg_ref[...] = lg


def _row_specs(d, tm, tiles_per_batch):
    vec = pl.BlockSpec((None, 1, d), lambda i: (i // tiles_per_batch, 0, 0))
    return [
        pl.BlockSpec((tm, d), lambda i: (i, 0)),
        pl.BlockSpec((1, d), lambda i: (0, 0)),
        vec,
        vec,
    ]


def _modulate(x2, g, shift, scale, seq):
    m, d = x2.shape
    tm = _pick(seq, 256, SUBLANES)
    return pl.pallas_call(
        _modulate_kernel,
        grid=(m // tm,),
        in_specs=_row_specs(d, tm, seq // tm),
        out_specs=pl.BlockSpec((tm, d), lambda i: (i, 0)),
        out_shape=jax.ShapeDtypeStruct((m, d), BF16),
        compiler_params=_params("parallel"),
        name="modulate",
    )(x2, g.reshape(1, d), shift, scale)


def _modulate_router(x2, g, shift, scale, w_router, seq):
    m, d = x2.shape
    n_exp = w_router.shape[1]
    tm = _pick(seq, 256, SUBLANES)
    w_pad = jnp.zeros((d, LANES), F32).at[:, :n_exp].set(w_router)
    w_hi = w_pad.astype(BF16)
    w_lo = (w_pad - w_hi.astype(F32)).astype(BF16)
    wspec = pl.BlockSpec((d, LANES), lambda i: (0, 0))
    return pl.pallas_call(
        _modulate_router_kernel,
        grid=(m // tm,),
        in_specs=_row_specs(d, tm, seq // tm) + [wspec, wspec],
        out_specs=[pl.BlockSpec((tm, d), lambda i: (i, 0)),
                   pl.BlockSpec((tm, LANES), lambda i: (i, 0))],
        out_shape=[jax.ShapeDtypeStruct((m, d), BF16),
                   jax.ShapeDtypeStruct((m, LANES), F32)],
        compiler_params=_params("parallel"),
        name="modulate_router",
    )(x2, g.reshape(1, d), shift, scale, w_hi, w_lo)


def _mm(a, w_ref):
    return jnp.dot(a, w_ref[...].astype(BF16), preferred_element_type=F32)


def _rope_rotate(v, first_half):
    return jnp.where(first_half, pltpu.roll(v, LANES - 32, 1), pltpu.roll(v, 32, 1))


def _qkv_kernel(a_ref, w_ref, gain_ref, cos_ref, sin_ref, o_ref):
    half = a_ref.shape[0] // QKV_ROW_SPLITS
    lane = lax.broadcasted_iota(jnp.int32, (half, HEAD_DIM), 1)
    first_half = (lane % 64) < 32
    for r in range(QKV_ROW_SPLITS):
        rows = slice(r * half, (r + 1) * half)
        acc = _mm(a_ref[rows, :], w_ref)
        cos = cos_ref[rows, :]
        sin = sin_ref[rows, :]
        for h in range(acc.shape[1] // HEAD_DIM):
            sl = slice(h * HEAD_DIM, (h + 1) * HEAD_DIM)
            v = acc[:, sl]
            norm = lax.rsqrt(jnp.mean(v * v, axis=-1, keepdims=True) + EPS) * gain_ref[0:1, sl]
            v = v * jnp.where(gain_ref[1:2, sl] > 0.0, norm, 1.0)
            o_ref[rows, sl] = (v * cos + _rope_rotate(v, first_half) * sin).astype(o_ref.dtype)


def _qkv_proj(h, w_qkv, q_gain, k_gain, cos, sin, seq, n_q, n_kv):
    m, d = h.shape
    n = w_qkv.shape[1]
    tm = _pick(seq, 1024, SUBLANES)
    tn = _pick(n_kv * HEAD_DIM, 512, LANES)
    heads_per_tile = tn // HEAD_DIM
    n_q_tiles = n_q * HEAD_DIM // tn
    n_k_tiles = n_kv * HEAD_DIM // tn
    n_rope_tiles = n_q_tiles + n_k_tiles
    qg = jnp.tile(q_gain * (HEAD_DIM ** -0.5 * LOG2_E), heads_per_tile)
    kg = jnp.tile(k_gain, heads_per_tile)
    gains = jnp.concatenate([jnp.broadcast_to(qg, (n_q_tiles, tn)),
                             jnp.broadcast_to(kg, (n_k_tiles, tn)),
                             jnp.ones((n_k_tiles, tn), F32)])
    normed = jnp.broadcast_to((jnp.arange(n // tn) < n_rope_tiles).astype(F32)[:, None],
                              (n // tn, tn))
    gains = jnp.stack([gains, normed], axis=1)
    cos2 = jnp.stack([cos, jnp.ones_like(cos)])
    sin2 = jnp.stack([sin, jnp.zeros_like(sin)])
    tiles_per_seq = seq // tm
    tab = pl.BlockSpec((None, tm, HEAD_DIM),
                       lambda j, i: ((j >= n_rope_tiles).astype(jnp.int32), i % tiles_per_seq, 0))
    return pl.pallas_call(
        _qkv_kernel,
        grid=(n // tn, m // tm),
        in_specs=[
            pl.BlockSpec((tm, d), lambda j, i: (i, 0)),
            pl.BlockSpec((d, tn), lambda j, i: (0, j)),
            pl.BlockSpec((None, 2, tn), lambda j, i: (j, 0, 0)),
            tab,
            tab,
        ],
        out_specs=pl.BlockSpec((tm, tn), lambda j, i: (i, j)),
        out_shape=jax.ShapeDtypeStruct((m, n), BF16),
        compiler_params=_params("arbitrary", "arbitrary"),
        name="qkv_proj",
    )(h, w_qkv, gains, cos2, sin2)


def _gelu_kernel(a_ref, w_ref, o_ref):
    slab = a_ref.shape[0] // GELU_ROW_SPLITS
    for r in range(GELU_ROW_SPLITS):
        rows = slice(r * slab, (r + 1) * slab)
        acc = _mm(a_ref[rows, :], w_ref)
        o_ref[rows, :] = (0.5 * acc * (1.0 + lax.erf(acc * (2.0 ** -0.5)))).astype(o_ref.dtype)


def _gelu_proj(h, w):
    m, d = h.shape
    n = w.shape[1]
    tm = _pick(m, 1024, SUBLANES)
    tn = _pick(n, 512, LANES)
    return pl.pallas_call(
        _gelu_kernel,
        grid=(n // tn, m // tm),
        in_specs=[pl.BlockSpec((tm, d), lambda j, i: (i, 0)),
                  pl.BlockSpec((d, tn), lambda j, i: (0, j))],
        out_specs=pl.BlockSpec((tm, tn), lambda j, i: (i, j)),
        out_shape=jax.ShapeDtypeStruct((m, n), BF16),
        compiler_params=_params("arbitrary", "arbitrary"),
        name="gelu_proj",
    )(h, w)


def _residual_kernel(a_ref, w_ref, x_ref, g_ref, o_ref):
    o_ref[...] = x_ref[...] + g_ref[...] * _mm(a_ref[...], w_ref)


def _residual_proj(a, w, x2, gate, seq, tm_pref, tn_pref, single_buffer_w):
    m, k = a.shape
    n = w.shape[1]
    tm = _pick(seq, tm_pref, SUBLANES)
    tn = _pick(n, tn_pref, LANES)
    tiles_per_batch = seq // tm
    w_mode = dict(pipeline_mode=pl.Buffered(1)) if single_buffer_w else {}
    return pl.pallas_call(
        _residual_kernel,
        grid=(n // tn, m // tm),
        in_specs=[
            pl.BlockSpec((tm, k), lambda j, i: (i, 0)),
            pl.BlockSpec((k, tn), lambda j, i: (0, j), **w_mode),
            pl.BlockSpec((tm, tn), lambda j, i: (i, j)),
            pl.BlockSpec((None, 1, tn), lambda j, i: (i // tiles_per_batch, 0, j)),
        ],
        out_specs=pl.BlockSpec((tm, tn), lambda j, i: (i, j)),
        out_shape=jax.ShapeDtypeStruct((m, n), F32),
        compiler_params=_params("arbitrary", "arbitrary"),
        name="residual_proj",
    )(a, w, x2, gate)


def _swiglu_kernel(a_ref, wg_ref, wu_ref, o_ref):
    a = a_ref[...]
    o_ref[...] = (_silu(_mm(a, wg_ref)) * _mm(a, wu_ref)).astype(o_ref.dtype)


def _swiglu_proj(h, w_gate, w_up):
    m, d = h.shape
    n = w_gate.shape[1]
    tm = _pick(m, 2048, SUBLANES)
    tn = _pick(n, 256, LANES)
    wspec = pl.BlockSpec((d, tn), lambda j, i: (0, j))
    return pl.pallas_call(
        _swiglu_kernel,
        grid=(n // tn, m // tm),
        in_specs=[pl.BlockSpec((tm, d), lambda j, i: (i, 0)), wspec, wspec],
        out_specs=pl.BlockSpec((tm, tn), lambda j, i: (i, j)),
        out_shape=jax.ShapeDtypeStruct((m, n), BF16),
        compiler_params=_params("arbitrary", "arbitrary"),
        name="swiglu_proj",
    )(h, w_gate, w_up)


def _attn_kernel(q_ref, k_ref, v_ref, o_ref, m_ref, l_ref, acc_ref, *, tk):
    tq = q_ref.shape[0]
    seq = k_ref.shape[0]
    lane_tiles = tk // LANES
    q = jnp.concatenate(
        [q_ref[:, g * HEAD_DIM:(g + 1) * HEAD_DIM] for g in range(GQA_GROUP)], axis=0)
    m_ref[...] = jnp.full_like(m_ref, -jnp.inf)
    l_ref[...] = jnp.zeros_like(l_ref)
    acc_ref[...] = jnp.zeros_like(acc_ref)

    def body(c, carry):
        start = pl.multiple_of(c * tk, tk)
        k = k_ref[pl.ds(start, tk), :]
        v = v_ref[pl.ds(start, tk), :]
        s = lax.dot_general(q, k, (((1,), (1,)), ((), ())), preferred_element_type=F32)
        tiles = [s[:, t * LANES:(t + 1) * LANES] for t in range(lane_tiles)]
        tile_max = functools.reduce(jnp.maximum, tiles)
        m_prev = m_ref[...]
        m_new = jnp.maximum(m_prev, jnp.max(tile_max, axis=-1, keepdims=True))
        alpha = jnp.exp2(m_prev - m_new)
        p = [jnp.exp2(t - m_new) for t in tiles]
        l_ref[...] = alpha * l_ref[...] + functools.reduce(jnp.add, p)
        pb = jnp.concatenate([t.astype(BF16) for t in p], axis=1)
        acc_ref[...] = alpha * acc_ref[...] + jnp.dot(pb, v, preferred_element_type=F32)
        m_ref[...] = m_new
        return carry

    lax.fori_loop(0, seq // tk, body, 0, unroll=True)
    out = acc_ref[...] / jnp.sum(l_ref[...], axis=-1, keepdims=True)
    for g in range(GQA_GROUP):
        o_ref[:, g * HEAD_DIM:(g + 1) * HEAD_DIM] = out[g * tq:(g + 1) * tq].astype(o_ref.dtype)


def _attention(qkv, batch, seq, n_q, n_kv):
    m = qkv.shape[0]
    tq = _pick(seq, 256, 16)
    tk = _pick(seq, 1024, LANES)
    group_w = GQA_GROUP * HEAD_DIM
    q_tiles = seq // tq
    return pl.pallas_call(
        functools.partial(_attn_kernel, tk=tk),
        grid=(batch, n_kv, q_tiles),
        in_specs=[
            pl.BlockSpec((tq, group_w), lambda b, j, i: (b * q_tiles + i, j)),
            pl.BlockSpec((seq, HEAD_DIM), lambda b, j, i: (b, n_q + j)),
            pl.BlockSpec((seq, HEAD_DIM), lambda b, j, i: (b, n_q + n_kv + j)),
        ],
        out_specs=pl.BlockSpec((tq, group_w), lambda b, j, i: (b * q_tiles + i, j)),
        out_shape=jax.ShapeDtypeStruct((m, n_q * HEAD_DIM), BF16),
        scratch_shapes=[pltpu.VMEM((GQA_GROUP * tq, LANES), F32),
                        pltpu.VMEM((GQA_GROUP * tq, LANES), F32),
                        pltpu.VMEM((GQA_GROUP * tq, HEAD_DIM), F32)],
        compiler_params=_params("parallel", "parallel", "arbitrary"),
        name="attention",
    )(qkv, qkv, qkv)


def _sgu_kernel(u_ref, v_ref, gain_ref, ws_ref, bias_ref, o_ref, wsb_ref):
    @pl.when(pl.program_id(0) == 0)
    def _():
        wsb_ref[...] = ws_ref[...].astype(BF16)

    rows = u_ref.shape[0]
    groups = ws_ref.shape[0]
    for c in range(rows // CHUNK):
        rs = slice(c * CHUNK, (c + 1) * CHUNK)
        v = v_ref[rs, :].astype(F32)
        vn = v * lax.rsqrt(jnp.mean(v * v, axis=-1, keepdims=True) + EPS) * gain_ref[...]
        vn = vn.astype(BF16)
        for g in range(groups):
            cs = slice(g * LANES, (g + 1) * LANES)
            sv = jnp.dot(wsb_ref[g], vn[:, cs], preferred_element_type=F32) + bias_ref[g]
            o_ref[rs, cs] = (u_ref[rs, cs].astype(F32) * sv).astype(o_ref.dtype)


def _spatial_gate(z, v_gain, w_s, b_s):
    m, two_w = z.shape
    width = two_w // 2
    groups = w_s.shape[0]
    rows = _pick(m, 2 * CHUNK, CHUNK)
    bias = jnp.broadcast_to(b_s[:, :, None], (groups, CHUNK, LANES))
    full = lambda shape: pl.BlockSpec(shape, lambda i: (0,) * len(shape))
    return pl.pallas_call(
        _sgu_kernel,
        grid=(m // rows,),
        in_specs=[
            pl.BlockSpec((rows, width), lambda i: (i, 0)),
            pl.BlockSpec((rows, width), lambda i: (i, 1)),
            full((1, width)),
            full((groups, CHUNK, CHUNK)),
            full((groups, CHUNK, LANES)),
        ],
        out_specs=pl.BlockSpec((rows, width), lambda i: (i, 0)),
        out_shape=jax.ShapeDtypeStruct((m, width), BF16),
        scratch_shapes=[pltpu.VMEM((groups, CHUNK, CHUNK), BF16)],
        compiler_params=_params("arbitrary"),
        name="spatial_gate",
    )(z, z, v_gain.reshape(1, width), w_s, bias)


_I1, _I2, _W1, _W2, _R1, _R2 = range(6)


def _route_kernel(lg_ref, info_ref, cnt_ref, carry_ref, *, n_exp):
    @pl.when(pl.program_id(0) == 0)
    def _():
        carry_ref[...] = jnp.zeros_like(carry_ref)

    lg = lg_ref[...]
    tb = lg.shape[0]
    lane = lax.broadcasted_iota(jnp.int32, lg.shape, 1).astype(F32)
    lg = jnp.where(lane < n_exp, lg, -jnp.inf)
    v1 = jnp.max(lg, axis=-1, keepdims=True)
    i1 = jnp.min(jnp.where(lg == v1, lane, float(LANES)), axis=-1, keepdims=True)
    rest = jnp.where(lane == i1, -jnp.inf, lg)
    v2 = jnp.max(rest, axis=-1, keepdims=True)
    i2 = jnp.min(jnp.where(rest == v2, lane, float(LANES)), axis=-1, keepdims=True)
    e = jnp.exp(v2 - v1)
    w1 = 1.0 / (1.0 + e)
    w2 = e / (1.0 + e)
    sel = jnp.where(lane == i1, 1.0, jnp.where(lane == i2, 1.0, 0.0))
    row = lax.broadcasted_iota(jnp.int32, (tb, tb), 0)
    col = lax.broadcasted_iota(jnp.int32, (tb, tb), 1)
    tri = jnp.where(row > col, 1.0, 0.0).astype(BF16)
    rank = jnp.dot(tri, sel.astype(BF16), preferred_element_type=F32) + carry_ref[0:1, :]
    r1 = jnp.sum(jnp.where(lane == i1, rank, 0.0), axis=-1, keepdims=True)
    r2 = jnp.sum(jnp.where(lane == i2, rank, 0.0), axis=-1, keepdims=True)
    carry_ref[...] += jnp.sum(sel, axis=0, keepdims=True)
    info = jnp.zeros_like(lg)
    for idx, val in ((_I1, i1), (_I2, i2), (_W1, w1), (_W2, w2), (_R1, r1), (_R2, r2)):
        info = jnp.where(lane == idx, val, info)
    info_ref[...] = info
    cnt_ref[...] = carry_ref[...]


def _route(logits, n_exp):
    n = logits.shape[0]
    tb = _pick(n, 512, SUBLANES)
    return pl.pallas_call(
        functools.partial(_route_kernel, n_exp=n_exp),
        grid=(n // tb,),
        in_specs=[pl.BlockSpec((tb, LANES), lambda i: (i, 0))],
        out_specs=[pl.BlockSpec((tb, LANES), lambda i: (i, 0)),
                   pl.BlockSpec((SUBLANES, LANES), lambda i: (0, 0))],
        out_shape=[jax.ShapeDtypeStruct((n, LANES), F32),
                   jax.ShapeDtypeStruct((SUBLANES, LANES), F32)],
        scratch_shapes=[pltpu.VMEM((SUBLANES, LANES), F32)],
        compiler_params=_params("arbitrary"),
        name="route",
    )(logits)


def _row_copy_out(h_ref, xs_ref, sem, t, dst_row):
    return pltpu.make_async_copy(h_ref.at[t], xs_ref.at[dst_row], sem)


def _dispatch_kernel(pos_ref, h_ref, zero_ref, xs_ref, sem):
    del zero_ref
    tt = h_ref.shape[0]
    base = pl.program_id(0) * tt * TOP_K

    def issue(t, carry):
        for k in range(TOP_K):
            _row_copy_out(h_ref, xs_ref, sem, t, pos_ref[base + t * TOP_K + k]).start()
        return carry

    def drain(t, carry):
        for k in range(TOP_K):
            _row_copy_out(h_ref, xs_ref, sem, t, 0).wait()
        return carry

    lax.fori_loop(0, tt, issue, 0)
    lax.fori_loop(0, tt, drain, 0)


def _dispatch(h, pos, n_rows):
    n, d = h.shape
    sub = d // LANES
    tt = _pick(n, 256, SUBLANES)
    h3 = h.reshape(n, sub, LANES)
    zeros = jnp.zeros((n_rows, sub, LANES), h.dtype)
    xs = pl.pallas_call(
        _dispatch_kernel,
        grid_spec=pltpu.PrefetchScalarGridSpec(
            num_scalar_prefetch=1,
            grid=(n // tt,),
            in_specs=[pl.BlockSpec((tt, sub, LANES), lambda i, pos: (i, 0, 0)),
                      pl.BlockSpec(memory_space=pl.ANY)],
            out_specs=pl.BlockSpec(memory_space=pl.ANY),
            scratch_shapes=[pltpu.SemaphoreType.DMA(())],
        ),
        out_shape=jax.ShapeDtypeStruct((n_rows, sub, LANES), h.dtype),
        input_output_aliases={2: 0},
        compiler_params=_params("arbitrary"),
        name="dispatch",
    )(pos.reshape(-1), h3, zeros)
    return xs.reshape(n_rows, d)


def _tile_is_live(i, nt_ref):
    return i < nt_ref[0]


def _zero_dead_tile(o_ref, nt_ref):
    @pl.when(jnp.logical_not(_tile_is_live(pl.program_id(1), nt_ref)))
    def _():
        o_ref[...] = jnp.zeros_like(o_ref)


def _expert_swiglu_kernel(te_ref, nt_ref, a_ref, wg_ref, wu_ref, o_ref):
    del te_ref
    _zero_dead_tile(o_ref, nt_ref)

    @pl.when(_tile_is_live(pl.program_id(1), nt_ref))
    def _():
        a = a_ref[...]
        o_ref[...] = (_silu(_mm(a, wg_ref)) * _mm(a, wu_ref)).astype(o_ref.dtype)


def _expert_down_kernel(te_ref, nt_ref, a_ref, w_ref, o_ref):
    del te_ref
    _zero_dead_tile(o_ref, nt_ref)

    @pl.when(_tile_is_live(pl.program_id(1), nt_ref))
    def _():
        o_ref[...] = _mm(a_ref[...], w_ref)


def _live_tile(i, nt):
    return jnp.minimum(i, nt[0] - 1)


def _expert_swiglu(xs, w_gate, w_up, tile_expert, n_tiles, tm):
    rows, d = xs.shape
    n = w_gate.shape[2]
    tn = _pick(n, 512, LANES)
    wspec = pl.BlockSpec((None, d, tn), lambda j, i, te, nt: (te[_live_tile(i, nt)], 0, j))
    return pl.pallas_call(
        _expert_swiglu_kernel,
        grid_spec=pltpu.PrefetchScalarGridSpec(
            num_scalar_prefetch=2,
            grid=(n // tn, rows // tm),
            in_specs=[pl.BlockSpec((tm, d), lambda j, i, te, nt: (_live_tile(i, nt), 0)),
                      wspec, wspec],
            out_specs=pl.BlockSpec((tm, tn), lambda j, i, te, nt: (i, j)),
        ),
        out_shape=jax.ShapeDtypeStruct((rows, n), BF16),
        compiler_params=_params("arbitrary", "arbitrary"),
        name="expert_swiglu",
    )(tile_expert, n_tiles, xs, w_gate, w_up)


def _expert_down(act, w_down, tile_expert, n_tiles, tm):
    rows, k = act.shape
    n = w_down.shape[2]
    tn = _pick(n, 1024, LANES)
    return pl.pallas_call(
        _expert_down_kernel,
        grid_spec=pltpu.PrefetchScalarGridSpec(
            num_scalar_prefetch=2,
            grid=(n // tn, rows // tm),
            in_specs=[pl.BlockSpec((tm, k), lambda j, i, te, nt: (_live_tile(i, nt), 0)),
                      pl.BlockSpec((None, k, tn),
                                   lambda j, i, te, nt: (te[_live_tile(i, nt)], 0, j))],
            out_specs=pl.BlockSpec((tm, tn), lambda j, i, te, nt: (i, j)),
        ),
        out_shape=jax.ShapeDtypeStruct((rows, n), F32),
        compiler_params=_params("arbitrary", "arbitrary"),
        name="expert_down",
    )(tile_expert, n_tiles, act, w_down)


def _row_copy_in(y_ref, buf_ref, sems, slot, k, t, src_row):
    return pltpu.make_async_copy(y_ref.at[pl.ds(src_row, 1), :],
                                 buf_ref.at[slot, k, pl.ds(t, 1), :], sems.at[slot])


def _combine_kernel(pos_ref, x_ref, g_ref, info_ref, fg_ref, y_ref, o_ref, buf_ref, sems):
    tt = x_ref.shape[0]
    i = pl.program_id(0)
    slot = i % 2

    def start_gather(step, into):
        base = step * tt * TOP_K

        def issue(t, carry):
            for k in range(TOP_K):
                _row_copy_in(y_ref, buf_ref, sems, into, k, t,
                             pos_ref[base + t * TOP_K + k]).start()
            return carry

        lax.fori_loop(0, tt, issue, 0)

    @pl.when(i == 0)
    def _():
        start_gather(0, 0)

    @pl.when(i + 1 < pl.num_programs(0))
    def _():
        start_gather(i + 1, 1 - slot)

    def drain(t, carry):
        for k in range(TOP_K):
            _row_copy_in(y_ref, buf_ref, sems, slot, k, t, 0).wait()
        return carry

    lax.fori_loop(0, tt, drain, 0)
    info = info_ref[...]
    f = info[:, _W1:_W1 + 1] * buf_ref[slot, 0] + info[:, _W2:_W2 + 1] * buf_ref[slot, 1]
    xn = x_ref[...] + g_ref[...] * f
    o_ref[...] = xn * lax.rsqrt(jnp.mean(xn * xn, axis=-1, keepdims=True) + EPS) * fg_ref[...]


def _combine_norm(x2, gate, info, pos, y, final_g, seq):
    n, d = x2.shape
    tt = _pick(seq, 128, SUBLANES)
    tiles_per_batch = seq // tt
    return pl.pallas_call(
        _combine_kernel,
        grid_spec=pltpu.PrefetchScalarGridSpec(
            num_scalar_prefetch=1,
            grid=(n // tt,),
            in_specs=[pl.BlockSpec((tt, d), lambda i, pos: (i, 0)),
                      pl.BlockSpec((None, 1, d), lambda i, pos: (i // tiles_per_batch, 0, 0)),
                      pl.BlockSpec((tt, LANES), lambda i, pos: (i, 0)),
                      pl.BlockSpec((1, d), lambda i, pos: (0, 0)),
                      pl.BlockSpec(memory_space=pl.ANY)],
            out_specs=pl.BlockSpec((tt, d), lambda i, pos: (i, 0)),
            scratch_shapes=[pltpu.VMEM((2, TOP_K, tt, d), F32),
                            pltpu.SemaphoreType.DMA((2,))],
        ),
        out_shape=jax.ShapeDtypeStruct((n, d), F32),
        compiler_params=_params("arbitrary"),
        name="combine_norm",
    )(pos.reshape(-1), x2, gate, info, final_g.reshape(1, d), y)


def _moe_layer(x2, g, shift, scale, gate, w_router, w_gate, w_up, w_down, final_g, seq):
    n = x2.shape[0]
    n_exp = w_router.shape[1]
    h, logits = _modulate_router(x2, g, shift, scale, w_router, seq)
    info, cnt = _route(logits, n_exp)

    tm = _pick(n, 512, SUBLANES)
    max_tiles = n * TOP_K // tm + n_exp
    counts = cnt[0, :n_exp].astype(jnp.int32)
    tiles_per_exp = (counts + tm - 1) // tm
    tile_end = jnp.cumsum(tiles_per_exp)
    offsets = (tile_end - tiles_per_exp) * tm
    n_tiles = tile_end[-1:]
    tile_ids = jnp.arange(max_tiles, dtype=jnp.int32)
    tile_expert = jnp.minimum(
        jnp.sum((tile_ids[:, None] >= tile_end[None, :]).astype(jnp.int32), axis=1), n_exp - 1)
    idx = info[:, _I1:_I2 + 1].astype(jnp.int32)
    pos = offsets[idx] + info[:, _R1:_R2 + 1].astype(jnp.int32)

    xs = _dispatch(h, pos, max_tiles * tm)
    act = _expert_swiglu(xs, w_gate, w_up, tile_expert, n_tiles, tm)
    y = _expert_down(act, w_down, tile_expert, n_tiles, tm)
    return _combine_norm(x2, gate, info, pos, y, final_g, seq)


def _rope_tables(seq):
    n_freq = HEAD_DIM // 4
    t = jnp.arange(seq, dtype=jnp.int32)
    inv_freq = 1.0 / (ROPE_THETA ** (jnp.arange(n_freq, dtype=F32) * 2.0 / (HEAD_DIM // 2)))
    row = (t // GRID_W).astype(F32)[:, None] * inv_freq
    col = (t % GRID_W).astype(F32)[:, None] * inv_freq
    cos = jnp.concatenate([jnp.cos(row), jnp.cos(row), jnp.cos(col), jnp.cos(col)], axis=1)
    sin = jnp.concatenate([-jnp.sin(row), jnp.sin(row), -jnp.sin(col), jnp.sin(col)], axis=1)
    return cos, sin


def kernel(x, c, w_mod, b_mod, norm_g, final_g, attn_w_qkv, attn_q_gain, attn_k_gain, attn_w_o, mix_w_uv, mix_v_gain, mix_w_s, mix_b_s, mix_w_out, ffn_w_gate, ffn_w_up, ffn_w_down, moe_w_router, moe_w_gate, moe_w_up, moe_w_down):
    batch, seq, d = x.shape
    depth = w_mod.shape[0]
    assert depth == 2 and d % (GQA_GROUP * HEAD_DIM) == 0 and seq % CHUNK == 0
    n_q = d // HEAD_DIM
    n_kv = n_q // GQA_GROUP
    cos, sin = _rope_tables(seq)

    mod = _adaln_mod(c, w_mod, b_mod).reshape(depth, batch, 6, 1, d)
    vec = lambda layer, which: mod[layer, :, which]
    x2 = x.reshape(batch * seq, d)

    h = _modulate(x2, norm_g[0, 0], vec(0, 0), vec(0, 1), seq)
    qkv = _qkv_proj(h, attn_w_qkv[0], attn_q_gain[0], attn_k_gain[0], cos, sin, seq, n_q, n_kv)
    o = _attention(qkv, batch, seq, n_q, n_kv)
    x2 = _residual_proj(o, attn_w_o[0], x2, vec(0, 2), seq, 1024, 512, False)
    h = _modulate(x2, norm_g[0, 1], vec(0, 3), vec(0, 4), seq)
    act = _swiglu_proj(h, ffn_w_gate[0], ffn_w_up[0])
    x2 = _residual_proj(act, ffn_w_down[0], x2, vec(0, 5), seq, 512, 512, True)

    h = _modulate(x2, norm_g[1, 0], vec(1, 0), vec(1, 1), seq)
    z = _gelu_proj(h, mix_w_uv[0])
    gated = _spatial_gate(z, mix_v_gain[0], mix_w_s[0], mix_b_s[0])
    x2 = _residual_proj(gated, mix_w_out[0], x2, vec(1, 2), seq, 1024, 512, False)
    out = _moe_layer(x2, norm_g[1, 1], vec(1, 3), vec(1, 4), vec(1, 5), moe_w_router[0],
                     moe_w_gate[0], moe_w_up[0], moe_w_down[0], final_g, seq)
    return out.reshape(batch, seq, d)
```

```python
import functools

import jax
import jax.numpy as jnp
from jax import lax
from jax.experimental import pallas as pl
from jax.experimental.pallas import tpu as pltpu

EPS = 1e-6
HEAD_DIM = 128
GQA_GROUP = 4
GRID_W = 64
ROPE_THETA = 10000.0
CHUNK = 128
TOP_K = 2
LOG2_E = 1.4426950408889634
LANES = 128
SUBLANES = 8
QKV_ROW_SPLITS = 2
VMEM_LIMIT_BYTES = 56 * 1024 * 1024

F32 = jnp.float32
BF16 = jnp.bfloat16


def _pick(dim, pref, mult):
    t = max(min(pref, dim) // mult * mult, mult)
    while t > mult and dim % t:
        t -= mult
    assert dim % t == 0, (dim, pref, mult)
    return t


def _params(*sem):
    return pltpu.CompilerParams(dimension_semantics=sem, vmem_limit_bytes=VMEM_LIMIT_BYTES)


def _silu(v):
    return v * jax.nn.sigmoid(v)


def _mod_kernel(c_ref, w_ref, b_ref, o_ref, acc_ref):
    k = pl.program_id(2)
    nb = c_ref.shape[0]
    tk, tn = w_ref.shape

    @pl.when(k == 0)
    def _():
        acc_ref[...] = jnp.zeros_like(acc_ref)

    for b in range(nb):
        cb = _silu(c_ref[b])
        parts = []
        for j in range(tn // LANES):
            prod = w_ref[:, j * LANES:(j + 1) * LANES] * cb
            parts.append(prod.reshape(tk // SUBLANES, SUBLANES, LANES).sum(axis=0))
        acc_ref[b] += jnp.concatenate(parts, axis=1)

    @pl.when(k == pl.num_programs(2) - 1)
    def _():
        for b in range(nb):
            o_ref[b:b + 1, :] = acc_ref[b].sum(axis=0, keepdims=True) + b_ref[...]


def _adaln_mod(c, w_mod, b_mod):
    depth, d, n = w_mod.shape
    nb = c.shape[0]
    tk = _pick(d, 512, SUBLANES)
    tn = _pick(n, 2048, LANES)
    c_rep = jnp.broadcast_to(c[:, :, None], (nb, d, LANES))
    return pl.pallas_call(
        _mod_kernel,
        grid=(depth, n // tn, d // tk),
        in_specs=[
            pl.BlockSpec((nb, tk, LANES), lambda l, j, k: (0, k, 0)),
            pl.BlockSpec((None, tk, tn), lambda l, j, k: (l, k, j)),
            pl.BlockSpec((None, 1, tn), lambda l, j, k: (l, 0, j)),
        ],
        out_specs=pl.BlockSpec((None, nb, tn), lambda l, j, k: (l, 0, j)),
        out_shape=jax.ShapeDtypeStruct((depth, nb, n), F32),
        scratch_shapes=[pltpu.VMEM((nb, SUBLANES, tn), F32)],
        compiler_params=_params("arbitrary", "arbitrary", "arbitrary"),
        name="adaln_mod",
    )(c_rep, w_mod, b_mod.reshape(depth, 1, n))


def _modulated(x_ref, g_ref, sh_ref, sc_ref):
    x = x_ref[...]
    y = x * lax.rsqrt(jnp.mean(x * x, axis=-1, keepdims=True) + EPS) * g_ref[...]
    return y * (1.0 + sc_ref[...]) + sh_ref[...]


def _modulate_kernel(x_ref, g_ref, sh_ref, sc_ref, o_ref):
    o_ref[...] = _modulated(x_ref, g_ref, sh_ref, sc_ref).astype(o_ref.dtype)


def _modulate_router_kernel(x_ref, g_ref, sh_ref, sc_ref, whi_ref, wlo_ref, o_ref, lg_ref):
    h = _modulated(x_ref, g_ref, sh_ref, sc_ref)
    hb = h.astype(BF16)
    o_ref[...] = hb
    h_lo = (h - hb.astype(F32)).astype(BF16)
    lg = jnp.dot(hb, whi_ref[...], preferred_element_type=F32)
    lg += jnp.dot(h_lo, whi_ref[...], preferred_element_type=F32)
    lg += jnp.dot(hb, wlo_ref[...], preferred_element_type=F32)
    lg_ref[...] = lg


def _row_specs(d, tm, tiles_per_batch):
    vec = pl.BlockSpec((None, 1, d), lambda i: (i // tiles_per_batch, 0, 0))
    return [
        pl.BlockSpec((tm, d), lambda i: (i, 0)),
        pl.BlockSpec((1, d), lambda i: (0, 0)),
        vec,
        vec,
    ]


def _modulate(x2, g, shift, scale, seq):
    m, d = x2.shape
    tm = _pick(seq, 256, SUBLANES)
    return pl.pallas_call(
        _modulate_kernel,
        grid=(m // tm,),
        in_specs=_row_specs(d, tm, seq // tm),
        out_specs=pl.BlockSpec((tm, d), lambda i: (i, 0)),
        out_shape=jax.ShapeDtypeStruct((m, d), BF16),
        compiler_params=_params("parallel"),
        name="modulate",
    )(x2, g.reshape(1, d), shift, scale)


def _modulate_router(x2, g, shift, scale, w_router, seq):
    m, d = x2.shape
    n_exp = w_router.shape[1]
    tm = _pick(seq, 256, SUBLANES)
    w_pad = jnp.zeros((d, LANES), F32).at[:, :n_exp].set(w_router)
    w_hi = w_pad.astype(BF16)
    w_lo = (w_pad - w_hi.astype(F32)).astype(BF16)
    wspec = pl.BlockSpec((d, LANES), lambda i: (0, 0))
    return pl.pallas_call(
        _modulate_router_kernel,
        grid=(m // tm,),
        in_specs=_row_specs(d, tm, seq // tm) + [wspec, wspec],
        out_specs=[pl.BlockSpec((tm, d), lambda i: (i, 0)),
                   pl.BlockSpec((tm, LANES), lambda i: (i, 0))],
        out_shape=[jax.ShapeDtypeStruct((m, d), BF16),
                   jax.ShapeDtypeStruct((m, LANES), F32)],
        compiler_params=_params("parallel"),
        name="modulate_router",
    )(x2, g.reshape(1, d), shift, scale, w_hi, w_lo)


def _mm(a, w_ref):
    return jnp.dot(a, w_ref[...].astype(BF16), preferred_element_type=F32)


def _rope_rotate(v, first_half):
    return jnp.where(first_half, pltpu.roll(v, LANES - 32, 1), pltpu.roll(v, 32, 1))


def _qkv_kernel(a_ref, w_ref, gain_ref, cos_ref, sin_ref, o_ref):
    half = a_ref.shape[0] // QKV_ROW_SPLITS
    lane = lax.broadcasted_iota(jnp.int32, (half, HEAD_DIM), 1)
    first_half = (lane % 64) < 32
    for r in range(QKV_ROW_SPLITS):
        rows = slice(r * half, (r + 1) * half)
        acc = _mm(a_ref[rows, :], w_ref)
        cos = cos_ref[rows, :]
        sin = sin_ref[rows, :]
        for h in range(acc.shape[1] // HEAD_DIM):
            sl = slice(h * HEAD_DIM, (h + 1) * HEAD_DIM)
            v = acc[:, sl]
            norm = lax.rsqrt(jnp.mean(v * v, axis=-1, keepdims=True) + EPS) * gain_ref[0:1, sl]
            v = v * jnp.where(gain_ref[1:2, sl] > 0.0, norm, 1.0)
            o_ref[rows, sl] = (v * cos + _rope_rotate(v, first_half) * sin).astype(o_ref.dtype)


def _qkv_proj(h, w_qkv, q_gain, k_gain, cos, sin, seq, n_q, n_kv):
    m, d = h.shape
    n = w_qkv.shape[1]
    tm = _pick(seq, 512, SUBLANES)
    tn = _pick(n_kv * HEAD_DIM, 1024, LANES)
    heads_per_tile = tn // HEAD_DIM
    n_q_tiles = n_q * HEAD_DIM // tn
    n_k_tiles = n_kv * HEAD_DIM // tn
    n_rope_tiles = n_q_tiles + n_k_tiles
    qg = jnp.tile(q_gain * (HEAD_DIM ** -0.5 * LOG2_E), heads_per_tile)
    kg = jnp.tile(k_gain, heads_per_tile)
    gains = jnp.concatenate([jnp.broadcast_to(qg, (n_q_tiles, tn)),
                             jnp.broadcast_to(kg, (n_k_tiles, tn)),
                             jnp.ones((n_k_tiles, tn), F32)])
    normed = jnp.broadcast_to((jnp.arange(n // tn) < n_rope_tiles).astype(F32)[:, None],
                              (n // tn, tn))
    gains = jnp.stack([gains, normed], axis=1)
    cos2 = jnp.stack([cos, jnp.ones_like(cos)])
    sin2 = jnp.stack([sin, jnp.zeros_like(sin)])
    tiles_per_seq = seq // tm
    tab = pl.BlockSpec((None, tm, HEAD_DIM),
                       lambda j, i: ((j >= n_rope_tiles).astype(jnp.int32), i % tiles_per_seq, 0))
    return pl.pallas_call(
        _qkv_kernel,
        grid=(n // tn, m // tm),
        in_specs=[
            pl.BlockSpec((tm, d), lambda j, i: (i, 0)),
            pl.BlockSpec((d, tn), lambda j, i: (0, j)),
            pl.BlockSpec((None, 2, tn), lambda j, i: (j, 0, 0)),
            tab,
            tab,
        ],
        out_specs=pl.BlockSpec((tm, tn), lambda j, i: (i, j)),
        out_shape=jax.ShapeDtypeStruct((m, n), BF16),
        compiler_params=_params("arbitrary", "arbitrary"),
        name="qkv_proj",
    )(h, w_qkv, gains, cos2, sin2)


def _gelu_kernel(a_ref, w_ref, o_ref):
    acc = _mm(a_ref[...], w_ref)
    o_ref[...] = (0.5 * acc * (1.0 + lax.erf(acc * (2.0 ** -0.5)))).astype(o_ref.dtype)


def _gelu_proj(h, w):
    m, d = h.shape
    n = w.shape[1]
    tm = _pick(m, 512, SUBLANES)
    tn = _pick(n, 1024, LANES)
    return pl.pallas_call(
        _gelu_kernel,
        grid=(n // tn, m // tm),
        in_specs=[pl.BlockSpec((tm, d), lambda j, i: (i, 0)),
                  pl.BlockSpec((d, tn), lambda j, i: (0, j))],
        out_specs=pl.BlockSpec((tm, tn), lambda j, i: (i, j)),
        out_shape=jax.ShapeDtypeStruct((m, n), BF16),
        compiler_params=_params("arbitrary", "arbitrary"),
        name="gelu_proj",
    )(h, w)


def _residual_kernel(a_ref, w_ref, x_ref, g_ref, o_ref):
    o_ref[...] = x_ref[...] + g_ref[...] * _mm(a_ref[...], w_ref)


def _residual_proj(a, w, x2, gate, seq, tm_pref, tn_pref, single_buffer_w):
    m, k = a.shape
    n = w.shape[1]
    tm = _pick(seq, tm_pref, SUBLANES)
    tn = _pick(n, tn_pref, LANES)
    tiles_per_batch = seq // tm
    w_mode = dict(pipeline_mode=pl.Buffered(1)) if single_buffer_w else {}
    return pl.pallas_call(
        _residual_kernel,
        grid=(n // tn, m // tm),
        in_specs=[
            pl.BlockSpec((tm, k), lambda j, i: (i, 0)),
            pl.BlockSpec((k, tn), lambda j, i: (0, j), **w_mode),
            pl.BlockSpec((tm, tn), lambda j, i: (i, j)),
            pl.BlockSpec((None, 1, tn), lambda j, i: (i // tiles_per_batch, 0, j)),
        ],
        out_specs=pl.BlockSpec((tm, tn), lambda j, i: (i, j)),
        out_shape=jax.ShapeDtypeStruct((m, n), F32),
        compiler_params=_params("arbitrary", "arbitrary"),
        name="residual_proj",
    )(a, w, x2, gate)


def _swiglu_kernel(a_ref, wg_ref, wu_ref, o_ref):
    a = a_ref[...]
    o_ref[...] = (_silu(_mm(a, wg_ref)) * _mm(a, wu_ref)).astype(o_ref.dtype)


def _swiglu_proj(h, w_gate, w_up):
    m, d = h.shape
    n = w_gate.shape[1]
    tm = _pick(m, 2048, SUBLANES)
    tn = _pick(n, 256, LANES)
    wspec = pl.BlockSpec((d, tn), lambda j, i: (0, j))
    return pl.pallas_call(
        _swiglu_kernel,
        grid=(n // tn, m // tm),
        in_specs=[pl.BlockSpec((tm, d), lambda j, i: (i, 0)), wspec, wspec],
        out_specs=pl.BlockSpec((tm, tn), lambda j, i: (i, j)),
        out_shape=jax.ShapeDtypeStruct((m, n), BF16),
        compiler_params=_params("arbitrary", "arbitrary"),
        name="swiglu_proj",
    )(h, w_gate, w_up)


def _attn_kernel(q_ref, k_ref, v_ref, o_ref, m_ref, l_ref, acc_ref, *, tk):
    tq = q_ref.shape[0]
    seq = k_ref.shape[0]
    kv_heads = k_ref.shape[1] // HEAD_DIM
    lane_tiles = tk // LANES
    group_w = GQA_GROUP * HEAD_DIM
    qs = [jnp.concatenate([q_ref[:, j * group_w + g * HEAD_DIM:j * group_w + (g + 1) * HEAD_DIM]
                           for g in range(GQA_GROUP)], axis=0) for j in range(kv_heads)]
    m_ref[...] = jnp.full_like(m_ref, -jnp.inf)
    l_ref[...] = jnp.zeros_like(l_ref)
    acc_ref[...] = jnp.zeros_like(acc_ref)

    def body(c, carry):
        start = pl.multiple_of(c * tk, tk)
        for j in range(kv_heads):
            cols = slice(j * HEAD_DIM, (j + 1) * HEAD_DIM)
            k = k_ref[pl.ds(start, tk), cols]
            v = v_ref[pl.ds(start, tk), cols]
            s = lax.dot_general(qs[j], k, (((1,), (1,)), ((), ())), preferred_element_type=F32)
            tiles = [s[:, t * LANES:(t + 1) * LANES] for t in range(lane_tiles)]
            tile_max = functools.reduce(jnp.maximum, tiles)
            m_prev = m_ref[j]
            m_new = jnp.maximum(m_prev, jnp.max(tile_max, axis=-1, keepdims=True))
            alpha = jnp.exp2(m_prev - m_new)
            p = [jnp.exp2(t - m_new) for t in tiles]
            l_ref[j] = alpha * l_ref[j] + functools.reduce(jnp.add, p)
            pb = jnp.concatenate([t.astype(BF16) for t in p], axis=1)
            acc_ref[j] = alpha * acc_ref[j] + jnp.dot(pb, v, preferred_element_type=F32)
            m_ref[j] = m_new
        return carry

    lax.fori_loop(0, seq // tk, body, 0, unroll=True)
    for j in range(kv_heads):
        out = acc_ref[j] / jnp.sum(l_ref[j], axis=-1, keepdims=True)
        for g in range(GQA_GROUP):
            cols = slice(j * group_w + g * HEAD_DIM, j * group_w + (g + 1) * HEAD_DIM)
            o_ref[:, cols] = out[g * tq:(g + 1) * tq].astype(o_ref.dtype)


def _attention(qkv, batch, seq, n_q, n_kv):
    m = qkv.shape[0]
    tq = _pick(seq, 256, 16)
    tk = _pick(seq, 1024, LANES)
    kv_heads = 2 if n_kv % 2 == 0 else 1
    group_w = kv_heads * GQA_GROUP * HEAD_DIM
    kv_w = kv_heads * HEAD_DIM
    q_tiles = seq // tq
    k_block0 = n_q * HEAD_DIM // kv_w
    v_block0 = (n_q + n_kv) * HEAD_DIM // kv_w
    state = pltpu.VMEM((kv_heads, GQA_GROUP * tq, LANES), F32)
    return pl.pallas_call(
        functools.partial(_attn_kernel, tk=tk),
        grid=(batch, n_kv // kv_heads, q_tiles),
        in_specs=[
            pl.BlockSpec((tq, group_w), lambda b, j, i: (b * q_tiles + i, j)),
            pl.BlockSpec((seq, kv_w), lambda b, j, i: (b, k_block0 + j)),
            pl.BlockSpec((seq, kv_w), lambda b, j, i: (b, v_block0 + j)),
        ],
        out_specs=pl.BlockSpec((tq, group_w), lambda b, j, i: (b * q_tiles + i, j)),
        out_shape=jax.ShapeDtypeStruct((m, n_q * HEAD_DIM), BF16),
        scratch_shapes=[state, state, state],
        compiler_params=_params("parallel", "parallel", "arbitrary"),
        name="attention",
    )(qkv, qkv, qkv)


def _sgu_kernel(u_ref, v_ref, gain_ref, ws_ref, bias_ref, o_ref, wsb_ref):
    @pl.when(pl.program_id(0) == 0)
    def _():
        wsb_ref[...] = ws_ref[...].astype(BF16)

    rows = u_ref.shape[0]
    groups = ws_ref.shape[0]
    for c in range(rows // CHUNK):
        rs = slice(c * CHUNK, (c + 1) * CHUNK)
        v = v_ref[rs, :].astype(F32)
        vn = v * lax.rsqrt(jnp.mean(v * v, axis=-1, keepdims=True) + EPS) * gain_ref[...]
        vn = vn.astype(BF16)
        for g in range(groups):
            cs = slice(g * LANES, (g + 1) * LANES)
            sv = jnp.dot(wsb_ref[g], vn[:, cs], preferred_element_type=F32) + bias_ref[g]
            o_ref[rs, cs] = (u_ref[rs, cs].astype(F32) * sv).astype(o_ref.dtype)


def _spatial_gate(z, v_gain, w_s, b_s):
    m, two_w = z.shape
    width = two_w // 2
    groups = w_s.shape[0]
    rows = _pick(m, 2 * CHUNK, CHUNK)
    bias = jnp.broadcast_to(b_s[:, :, None], (groups, CHUNK, LANES))
    full = lambda shape: pl.BlockSpec(shape, lambda i: (0,) * len(shape))
    return pl.pallas_call(
        _sgu_kernel,
        grid=(m // rows,),
        in_specs=[
            pl.BlockSpec((rows, width), lambda i: (i, 0)),
            pl.BlockSpec((rows, width), lambda i: (i, 1)),
            full((1, width)),
            full((groups, CHUNK, CHUNK)),
            full((groups, CHUNK, LANES)),
        ],
        out_specs=pl.BlockSpec((rows, width), lambda i: (i, 0)),
        out_shape=jax.ShapeDtypeStruct((m, width), BF16),
        scratch_shapes=[pltpu.VMEM((groups, CHUNK, CHUNK), BF16)],
        compiler_params=_params("arbitrary"),
        name="spatial_gate",
    )(z, z, v_gain.reshape(1, width), w_s, bias)


_I1, _I2, _W1, _W2, _R1, _R2 = range(6)


def _route_kernel(lg_ref, info_ref, cnt_ref, carry_ref, *, n_exp):
    @pl.when(pl.program_id(0) == 0)
    def _():
        carry_ref[...] = jnp.zeros_like(carry_ref)

    lg = lg_ref[...]
    tb = lg.shape[0]
    lane = lax.broadcasted_iota(jnp.int32, lg.shape, 1).astype(F32)
    lg = jnp.where(lane < n_exp, lg, -jnp.inf)
    v1 = jnp.max(lg, axis=-1, keepdims=True)
    i1 = jnp.min(jnp.where(lg == v1, lane, float(LANES)), axis=-1, keepdims=True)
    rest = jnp.where(lane == i1, -jnp.inf, lg)
    v2 = jnp.max(rest, axis=-1, keepdims=True)
    i2 = jnp.min(jnp.where(rest == v2, lane, float(LANES)), axis=-1, keepdims=True)
    e = jnp.exp(v2 - v1)
    w1 = 1.0 / (1.0 + e)
    w2 = e / (1.0 + e)
    sel = jnp.where(lane == i1, 1.0, jnp.where(lane == i2, 1.0, 0.0))
    row = lax.broadcasted_iota(jnp.int32, (tb, tb), 0)
    col = lax.broadcasted_iota(jnp.int32, (tb, tb), 1)
    tri = jnp.where(row > col, 1.0, 0.0).astype(BF16)
    rank = jnp.dot(tri, sel.astype(BF16), preferred_element_type=F32) + carry_ref[0:1, :]
    r1 = jnp.sum(jnp.where(lane == i1, rank, 0.0), axis=-1, keepdims=True)
    r2 = jnp.sum(jnp.where(lane == i2, rank, 0.0), axis=-1, keepdims=True)
    carry_ref[...] += jnp.sum(sel, axis=0, keepdims=True)
    info = jnp.zeros_like(lg)
    for idx, val in ((_I1, i1), (_I2, i2), (_W1, w1), (_W2, w2), (_R1, r1), (_R2, r2)):
        info = jnp.where(lane == idx, val, info)
    info_ref[...] = info
    cnt_ref[...] = carry_ref[...]


def _route(logits, n_exp):
    n = logits.shape[0]
    tb = _pick(n, 512, SUBLANES)
    return pl.pallas_call(
        functools.partial(_route_kernel, n_exp=n_exp),
        grid=(n // tb,),
        in_specs=[pl.BlockSpec((tb, LANES), lambda i: (i, 0))],
        out_specs=[pl.BlockSpec((tb, LANES), lambda i: (i, 0)),
                   pl.BlockSpec((SUBLANES, LANES), lambda i: (0, 0))],
        out_shape=[jax.ShapeDtypeStruct((n, LANES), F32),
                   jax.ShapeDtypeStruct((SUBLANES, LANES), F32)],
        scratch_shapes=[pltpu.VMEM((SUBLANES, LANES), F32)],
        compiler_params=_params("arbitrary"),
        name="route",
    )(logits)


def _row_copy_out(h_ref, xs_ref, sem, t, dst_row):
    return pltpu.make_async_copy(h_ref.at[t], xs_ref.at[dst_row], sem)


def _dispatch_kernel(pos_ref, h_ref, zero_ref, xs_ref, sem):
    del zero_ref
    tt = h_ref.shape[0]
    base = pl.program_id(0) * tt * TOP_K

    def issue(t, carry):
        for k in range(TOP_K):
            _row_copy_out(h_ref, xs_ref, sem, t, pos_ref[base + t * TOP_K + k]).start()
        return carry

    def drain(t, carry):
        for k in range(TOP_K):
            _row_copy_out(h_ref, xs_ref, sem, t, 0).wait()
        return carry

    lax.fori_loop(0, tt, issue, 0)
    lax.fori_loop(0, tt, drain, 0)


def _dispatch(h, pos, n_rows):
    n, d = h.shape
    sub = d // LANES
    tt = _pick(n, 256, SUBLANES)
    h3 = h.reshape(n, sub, LANES)
    zeros = jnp.zeros((n_rows, sub, LANES), h.dtype)
    xs = pl.pallas_call(
        _dispatch_kernel,
        grid_spec=pltpu.PrefetchScalarGridSpec(
            num_scalar_prefetch=1,
            grid=(n // tt,),
            in_specs=[pl.BlockSpec((tt, sub, LANES), lambda i, pos: (i, 0, 0)),
                      pl.BlockSpec(memory_space=pl.ANY)],
            out_specs=pl.BlockSpec(memory_space=pl.ANY),
            scratch_shapes=[pltpu.SemaphoreType.DMA(())],
        ),
        out_shape=jax.ShapeDtypeStruct((n_rows, sub, LANES), h.dtype),
        input_output_aliases={2: 0},
        compiler_params=_params("arbitrary"),
        name="dispatch",
    )(pos.reshape(-1), h3, zeros)
    return xs.reshape(n_rows, d)


def _per_tile_rows(o_ref, rows_ref, compute):
    tm = o_ref.shape[0]
    half = tm // 2
    valid = rows_ref[pl.program_id(1)]

    @pl.when(valid > half)
    def _():
        o_ref[...] = compute(slice(0, tm))

    @pl.when((valid > 0) & (valid <= half))
    def _():
        o_ref[0:half, :] = compute(slice(0, half))
        o_ref[half:tm, :] = jnp.zeros((tm - half, o_ref.shape[1]), o_ref.dtype)

    @pl.when(valid == 0)
    def _():
        o_ref[...] = jnp.zeros_like(o_ref)


def _expert_swiglu_kernel(te_ref, nt_ref, rows_ref, a_ref, wg_ref, wu_ref, o_ref):
    del te_ref, nt_ref

    def compute(rows):
        a = a_ref[rows, :]
        return (_silu(_mm(a, wg_ref)) * _mm(a, wu_ref)).astype(o_ref.dtype)

    _per_tile_rows(o_ref, rows_ref, compute)


def _expert_down_kernel(te_ref, nt_ref, rows_ref, a_ref, w_ref, o_ref):
    del te_ref, nt_ref
    _per_tile_rows(o_ref, rows_ref, lambda rows: _mm(a_ref[rows, :], w_ref))


def _live_tile(i, nt):
    return jnp.minimum(i, nt[0] - 1)


def _expert_swiglu(xs, w_gate, w_up, tile_expert, n_tiles, tile_rows, tm):
    rows, d = xs.shape
    n = w_gate.shape[2]
    tn = _pick(n, 512, LANES)
    wspec = pl.BlockSpec((None, d, tn), lambda j, i, te, nt, tr: (te[_live_tile(i, nt)], 0, j))
    return pl.pallas_call(
        _expert_swiglu_kernel,
        grid_spec=pltpu.PrefetchScalarGridSpec(
            num_scalar_prefetch=3,
            grid=(n // tn, rows // tm),
            in_specs=[pl.BlockSpec((tm, d), lambda j, i, te, nt, tr: (_live_tile(i, nt), 0)),
                      wspec, wspec],
            out_specs=pl.BlockSpec((tm, tn), lambda j, i, te, nt, tr: (i, j)),
        ),
        out_shape=jax.ShapeDtypeStruct((rows, n), BF16),
        compiler_params=_params("arbitrary", "arbitrary"),
        name="expert_swiglu",
    )(tile_expert, n_tiles, tile_rows, xs, w_gate, w_up)


def _expert_down(act, w_down, tile_expert, n_tiles, tile_rows, tm):
    rows, k = act.shape
    n = w_down.shape[2]
    tn = _pick(n, 1024, LANES)
    return pl.pallas_call(
        _expert_down_kernel,
        grid_spec=pltpu.PrefetchScalarGridSpec(
            num_scalar_prefetch=3,
            grid=(n // tn, rows // tm),
            in_specs=[pl.BlockSpec((tm, k), lambda j, i, te, nt, tr: (_live_tile(i, nt), 0)),
                      pl.BlockSpec((None, k, tn),
                                   lambda j, i, te, nt, tr: (te[_live_tile(i, nt)], 0, j))],
            out_specs=pl.BlockSpec((tm, tn), lambda j, i, te, nt, tr: (i, j)),
        ),
        out_shape=jax.ShapeDtypeStruct((rows, n), F32),
        compiler_params=_params("arbitrary", "arbitrary"),
        name="expert_down",
    )(tile_expert, n_tiles, tile_rows, act, w_down)


def _row_copy_in(y_ref, buf_ref, sems, slot, k, t, src_row):
    return pltpu.make_async_copy(y_ref.at[pl.ds(src_row, 1), :],
                                 buf_ref.at[slot, k, pl.ds(t, 1), :], sems.at[slot])


def _combine_kernel(pos_ref, x_ref, g_ref, info_ref, fg_ref, y_ref, o_ref, buf_ref, sems):
    tt = x_ref.shape[0]
    i = pl.program_id(0)
    slot = i % 2

    def start_gather(step, into):
        base = step * tt * TOP_K

        def issue(t, carry):
            for k in range(TOP_K):
                _row_copy_in(y_ref, buf_ref, sems, into, k, t,
                             pos_ref[base + t * TOP_K + k]).start()
            return carry

        lax.fori_loop(0, tt, issue, 0)

    @pl.when(i == 0)
    def _():
        start_gather(0, 0)

    @pl.when(i + 1 < pl.num_programs(0))
    def _():
        start_gather(i + 1, 1 - slot)

    def drain(t, carry):
        for k in range(TOP_K):
            _row_copy_in(y_ref, buf_ref, sems, slot, k, t, 0).wait()
        return carry

    lax.fori_loop(0, tt, drain, 0)
    info = info_ref[...]
    f = info[:, _W1:_W1 + 1] * buf_ref[slot, 0] + info[:, _W2:_W2 + 1] * buf_ref[slot, 1]
    xn = x_ref[...] + g_ref[...] * f
    o_ref[...] = xn * lax.rsqrt(jnp.mean(xn * xn, axis=-1, keepdims=True) + EPS) * fg_ref[...]


def _combine_norm(x2, gate, info, pos, y, final_g, seq):
    n, d = x2.shape
    tt = _pick(seq, 128, SUBLANES)
    tiles_per_batch = seq // tt
    return pl.pallas_call(
        _combine_kernel,
        grid_spec=pltpu.PrefetchScalarGridSpec(
            num_scalar_prefetch=1,
            grid=(n // tt,),
            in_specs=[pl.BlockSpec((tt, d), lambda i, pos: (i, 0)),
                      pl.BlockSpec((None, 1, d), lambda i, pos: (i // tiles_per_batch, 0, 0)),
                      pl.BlockSpec((tt, LANES), lambda i, pos: (i, 0)),
                      pl.BlockSpec((1, d), lambda i, pos: (0, 0)),
                      pl.BlockSpec(memory_space=pl.ANY)],
            out_specs=pl.BlockSpec((tt, d), lambda i, pos: (i, 0)),
            scratch_shapes=[pltpu.VMEM((2, TOP_K, tt, d), F32),
                            pltpu.SemaphoreType.DMA((2,))],
        ),
        out_shape=jax.ShapeDtypeStruct((n, d), F32),
        compiler_params=_params("arbitrary"),
        name="combine_norm",
    )(pos.reshape(-1), x2, gate, info, final_g.reshape(1, d), y)


def _moe_layer(x2, g, shift, scale, gate, w_router, w_gate, w_up, w_down, final_g, seq):
    n = x2.shape[0]
    n_exp = w_router.shape[1]
    h, logits = _modulate_router(x2, g, shift, scale, w_router, seq)
    info, cnt = _route(logits, n_exp)

    tm = _pick(n, 512, SUBLANES)
    max_tiles = n * TOP_K // tm + n_exp
    counts = cnt[0, :n_exp].astype(jnp.int32)
    tiles_per_exp = (counts + tm - 1) // tm
    tile_end = jnp.cumsum(tiles_per_exp)
    offsets = (tile_end - tiles_per_exp) * tm
    n_tiles = tile_end[-1:]
    tile_ids = jnp.arange(max_tiles, dtype=jnp.int32)
    tile_expert = jnp.minimum(
        jnp.sum((tile_ids[:, None] >= tile_end[None, :]).astype(jnp.int32), axis=1), n_exp - 1)
    first_tile = (tile_end - tiles_per_exp)[tile_expert]
    tile_rows = jnp.clip(counts[tile_expert] - (tile_ids - first_tile) * tm, 0, tm)
    tile_rows = jnp.where(tile_ids < n_tiles[0], tile_rows, 0).astype(jnp.int32)
    idx = info[:, _I1:_I2 + 1].astype(jnp.int32)
    pos = offsets[idx] + info[:, _R1:_R2 + 1].astype(jnp.int32)

    xs = _dispatch(h, pos, max_tiles * tm)
    act = _expert_swiglu(xs, w_gate, w_up, tile_expert, n_tiles, tile_rows, tm)
    y = _expert_down(act, w_down, tile_expert, n_tiles, tile_rows, tm)
    return _combine_norm(x2, gate, info, pos, y, final_g, seq)


def _rope_tables(seq):
    n_freq = HEAD_DIM // 4
    t = jnp.arange(seq, dtype=jnp.int32)
    inv_freq = 1.0 / (ROPE_THETA ** (jnp.arange(n_freq, dtype=F32) * 2.0 / (HEAD_DIM // 2)))
    row = (t // GRID_W).astype(F32)[:, None] * inv_freq
    col = (t % GRID_W).astype(F32)[:, None] * inv_freq
    cos = jnp.concatenate([jnp.cos(row), jnp.cos(row), jnp.cos(col), jnp.cos(col)], axis=1)
    sin = jnp.concatenate([-jnp.sin(row), jnp.sin(row), -jnp.sin(col), jnp.sin(col)], axis=1)
    return cos, sin


def kernel(x, c, w_mod, b_mod, norm_g, final_g, attn_w_qkv, attn_q_gain, attn_k_gain, attn_w_o, mix_w_uv, mix_v_gain, mix_w_s, mix_b_s, mix_w_out, ffn_w_gate, ffn_w_up, ffn_w_down, moe_w_router, moe_w_gate, moe_w_up, moe_w_down):
    batch, seq, d = x.shape
    depth = w_mod.shape[0]
    assert depth == 2 and d % (GQA_GROUP * HEAD_DIM) == 0 and seq % CHUNK == 0
    n_q = d // HEAD_DIM
    n_kv = n_q // GQA_GROUP
    cos, sin = _rope_tables(seq)

    mod = _adaln_mod(c, w_mod, b_mod).reshape(depth, batch, 6, 1, d)
    vec = lambda layer, which: mod[layer, :, which]
    x2 = x.reshape(batch * seq, d)

    h = _modulate(x2, norm_g[0, 0], vec(0, 0), vec(0, 1), seq)
    qkv = _qkv_proj(h, attn_w_qkv[0], attn_q_gain[0], attn_k_gain[0], cos, sin, seq, n_q, n_kv)
    o = _attention(qkv, batch, seq, n_q, n_kv)
    x2 = _residual_proj(o, attn_w_o[0], x2, vec(0, 2), seq, 512, 1024, False)
    h = _modulate(x2, norm_g[0, 1], vec(0, 3), vec(0, 4), seq)
    act = _swiglu_proj(h, ffn_w_gate[0], ffn_w_up[0])
    x2 = _residual_proj(act, ffn_w_down[0], x2, vec(0, 5), seq, 512, 512, True)

    h = _modulate(x2, norm_g[1, 0], vec(1, 0), vec(1, 1), seq)
    z = _gelu_proj(h, mix_w_uv[0])
    gated = _spatial_gate(z, mix_v_gain[0], mix_w_s[0], mix_b_s[0])
    x2 = _residual_proj(gated, mix_w_out[0], x2, vec(1, 2), seq, 512, 1024, False)
    out = _moe_layer(x2, norm_g[1, 1], vec(1, 3), vec(1, 4), vec(1, 5), moe_w_router[0],
                     moe_w_gate[0], moe_w_up[0], moe_w_down[0], final_g, seq)
    return out.reshape(batch, seq, d)
```

```python
import functools

import jax
import jax.numpy as jnp
from jax import lax
from jax.experimental import pallas as pl
from jax.experimental.pallas import tpu as pltpu

EPS = 1e-6
HEAD_DIM = 128
GQA_GROUP = 4
GRID_W = 64
ROPE_THETA = 10000.0
CHUNK = 128
TOP_K = 2
LOG2_E = 1.4426950408889634
LANES = 128
SUBLANES = 8
QKV_ROW_SPLITS = 2
VMEM_LIMIT_BYTES = 56 * 1024 * 1024

F32 = jnp.float32
BF16 = jnp.bfloat16


def _pick(dim, pref, mult):
    t = max(min(pref, dim) // mult * mult, mult)
    while t > mult and dim % t:
        t -= mult
    assert dim % t == 0, (dim, pref, mult)
    return t


def _params(*sem):
    return pltpu.CompilerParams(dimension_semantics=sem, vmem_limit_bytes=VMEM_LIMIT_BYTES)


def _silu(v):
    return v * jax.nn.sigmoid(v)


def _mod_kernel(c_ref, w_ref, b_ref, o_ref, acc_ref):
    k = pl.program_id(2)
    nb = c_ref.shape[0]
    tk, tn = w_ref.shape

    @pl.when(k == 0)
    def _():
        acc_ref[...] = jnp.zeros_like(acc_ref)

    for b in range(nb):
        cb = _silu(c_ref[b])
        parts = []
        for j in range(tn // LANES):
            prod = w_ref[:, j * LANES:(j + 1) * LANES] * cb
            parts.append(prod.reshape(tk // SUBLANES, SUBLANES, LANES).sum(axis=0))
        acc_ref[b] += jnp.concatenate(parts, axis=1)

    @pl.when(k == pl.num_programs(2) - 1)
    def _():
        for b in range(nb):
            o_ref[b:b + 1, :] = acc_ref[b].sum(axis=0, keepdims=True) + b_ref[...]


def _adaln_mod(c, w_mod, b_mod):
    depth, d, n = w_mod.shape
    nb = c.shape[0]
    tk = _pick(d, 512, SUBLANES)
    tn = _pick(n, 2048, LANES)
    c_rep = jnp.broadcast_to(c[:, :, None], (nb, d, LANES))
    return pl.pallas_call(
        _mod_kernel,
        grid=(depth, n // tn, d // tk),
        in_specs=[
            pl.BlockSpec((nb, tk, LANES), lambda l, j, k: (0, k, 0)),
            pl.BlockSpec((None, tk, tn), lambda l, j, k: (l, k, j)),
            pl.BlockSpec((None, 1, tn), lambda l, j, k: (l, 0, j)),
        ],
        out_specs=pl.BlockSpec((None, nb, tn), lambda l, j, k: (l, 0, j)),
        out_shape=jax.ShapeDtypeStruct((depth, nb, n), F32),
        scratch_shapes=[pltpu.VMEM((nb, SUBLANES, tn), F32)],
        compiler_params=_params("arbitrary", "arbitrary", "arbitrary"),
        name="adaln_mod",
    )(c_rep, w_mod, b_mod.reshape(depth, 1, n))


def _modulated(x_ref, g_ref, sh_ref, sc_ref):
    x = x_ref[...]
    y = x * lax.rsqrt(jnp.mean(x * x, axis=-1, keepdims=True) + EPS) * g_ref[...]
    return y * (1.0 + sc_ref[...]) + sh_ref[...]


def _modulate_kernel(x_ref, g_ref, sh_ref, sc_ref, o_ref):
    o_ref[...] = _modulated(x_ref, g_ref, sh_ref, sc_ref).astype(o_ref.dtype)


def _modulate_router_kernel(x_ref, g_ref, sh_ref, sc_ref, whi_ref, wlo_ref, o_ref, lg_ref):
    h = _modulated(x_ref, g_ref, sh_ref, sc_ref)
    hb = h.astype(BF16)
    o_ref[...] = hb
    h_lo = (h - hb.astype(F32)).astype(BF16)
    lg = jnp.dot(hb, whi_ref[...], preferred_element_type=F32)
    lg += jnp.dot(h_lo, whi_ref[...], preferred_element_type=F32)
    lg += jnp.dot(hb, wlo_ref[...], preferred_element_type=F32)
    lg_ref[...] = lg


def _row_specs(d, tm, tiles_per_batch):
    vec = pl.BlockSpec((None, 1, d), lambda i: (i // tiles_per_batch, 0, 0))
    return [
        pl.BlockSpec((tm, d), lambda i: (i, 0)),
        pl.BlockSpec((1, d), lambda i: (0, 0)),
        vec,
        vec,
    ]


def _modulate(x2, g, shift, scale, seq):
    m, d = x2.shape
    tm = _pick(seq, 256, SUBLANES)
    return pl.pallas_call(
        _modulate_kernel,
        grid=(m // tm,),
        in_specs=_row_specs(d, tm, seq // tm),
        out_specs=pl.BlockSpec((tm, d), lambda i: (i, 0)),
        out_shape=jax.ShapeDtypeStruct((m, d), BF16),
        compiler_params=_params("parallel"),
        name="modulate",
    )(x2, g.reshape(1, d), shift, scale)


def _modulate_router(x2, g, shift, scale, w_router, seq):
    m, d = x2.shape
    n_exp = w_router.shape[1]
    tm = _pick(seq, 256, SUBLANES)
    w_pad = jnp.zeros((d, LANES), F32).at[:, :n_exp].set(w_router)
    w_hi = w_pad.astype(BF16)
    w_lo = (w_pad - w_hi.astype(F32)).astype(BF16)
    wspec = pl.BlockSpec((d, LANES), lambda i: (0, 0))
    return pl.pallas_call(
        _modulate_router_kernel,
        grid=(m // tm,),
        in_specs=_row_specs(d, tm, seq // tm) + [wspec, wspec],
        out_specs=[pl.BlockSpec((tm, d), lambda i: (i, 0)),
                   pl.BlockSpec((tm, LANES), lambda i: (i, 0))],
        out_shape=[jax.ShapeDtypeStruct((m, d), BF16),
                   jax.ShapeDtypeStruct((m, LANES), F32)],
        compiler_params=_params("parallel"),
        name="modulate_router",
    )(x2, g.reshape(1, d), shift, scale, w_hi, w_lo)


def _mm(a, w_ref):
    return jnp.dot(a, w_ref[...].astype(BF16), preferred_element_type=F32)


def _rope_rotate(v, first_half):
    return jnp.where(first_half, pltpu.roll(v, LANES - 32, 1), pltpu.roll(v, 32, 1))


def _qkv_kernel(a_ref, w_ref, gain_ref, cos_ref, sin_ref, o_ref):
    half = a_ref.shape[0] // QKV_ROW_SPLITS
    lane = lax.broadcasted_iota(jnp.int32, (half, HEAD_DIM), 1)
    first_half = (lane % 64) < 32
    for r in range(QKV_ROW_SPLITS):
        rows = slice(r * half, (r + 1) * half)
        acc = _mm(a_ref[rows, :], w_ref)
        cos = cos_ref[rows, :]
        sin = sin_ref[rows, :]
        for h in range(acc.shape[1] // HEAD_DIM):
            sl = slice(h * HEAD_DIM, (h + 1) * HEAD_DIM)
            v = acc[:, sl]
            norm = lax.rsqrt(jnp.mean(v * v, axis=-1, keepdims=True) + EPS) * gain_ref[0:1, sl]
            v = v * jnp.where(gain_ref[1:2, sl] > 0.0, norm, 1.0)
            o_ref[rows, sl] = (v * cos + _rope_rotate(v, first_half) * sin).astype(o_ref.dtype)


def _qkv_proj(h, w_qkv, q_gain, k_gain, cos, sin, seq, n_q, n_kv):
    m, d = h.shape
    n = w_qkv.shape[1]
    tm = _pick(seq, 512, SUBLANES)
    tn = _pick(n_kv * HEAD_DIM, 1024, LANES)
    heads_per_tile = tn // HEAD_DIM
    n_q_tiles = n_q * HEAD_DIM // tn
    n_k_tiles = n_kv * HEAD_DIM // tn
    n_rope_tiles = n_q_tiles + n_k_tiles
    qg = jnp.tile(q_gain * (HEAD_DIM ** -0.5 * LOG2_E), heads_per_tile)
    kg = jnp.tile(k_gain, heads_per_tile)
    gains = jnp.concatenate([jnp.broadcast_to(qg, (n_q_tiles, tn)),
                             jnp.broadcast_to(kg, (n_k_tiles, tn)),
                             jnp.ones((n_k_tiles, tn), F32)])
    normed = jnp.broadcast_to((jnp.arange(n // tn) < n_rope_tiles).astype(F32)[:, None],
                              (n // tn, tn))
    gains = jnp.stack([gains, normed], axis=1)
    cos2 = jnp.stack([cos, jnp.ones_like(cos)])
    sin2 = jnp.stack([sin, jnp.zeros_like(sin)])
    tiles_per_seq = seq // tm
    tab = pl.BlockSpec((None, tm, HEAD_DIM),
                       lambda j, i: ((j >= n_rope_tiles).astype(jnp.int32), i % tiles_per_seq, 0))
    return pl.pallas_call(
        _qkv_kernel,
        grid=(n // tn, m // tm),
        in_specs=[
            pl.BlockSpec((tm, d), lambda j, i: (i, 0)),
            pl.BlockSpec((d, tn), lambda j, i: (0, j)),
            pl.BlockSpec((None, 2, tn), lambda j, i: (j, 0, 0)),
            tab,
            tab,
        ],
        out_specs=pl.BlockSpec((tm, tn), lambda j, i: (i, j)),
        out_shape=jax.ShapeDtypeStruct((m, n), BF16),
        compiler_params=_params("arbitrary", "arbitrary"),
        name="qkv_proj",
    )(h, w_qkv, gains, cos2, sin2)


def _gelu_kernel(a_ref, w_ref, o_ref):
    acc = _mm(a_ref[...], w_ref)
    o_ref[...] = (0.5 * acc * (1.0 + lax.erf(acc * (2.0 ** -0.5)))).astype(o_ref.dtype)


def _gelu_proj(h, w):
    m, d = h.shape
    n = w.shape[1]
    tm = _pick(m, 512, SUBLANES)
    tn = _pick(n, 1024, LANES)
    return pl.pallas_call(
        _gelu_kernel,
        grid=(n // tn, m // tm),
        in_specs=[pl.BlockSpec((tm, d), lambda j, i: (i, 0)),
                  pl.BlockSpec((d, tn), lambda j, i: (0, j))],
        out_specs=pl.BlockSpec((tm, tn), lambda j, i: (i, j)),
        out_shape=jax.ShapeDtypeStruct((m, n), BF16),
        compiler_params=_params("arbitrary", "arbitrary"),
        name="gelu_proj",
    )(h, w)


def _residual_kernel(a_ref, w_ref, x_ref, g_ref, o_ref):
    o_ref[...] = x_ref[...] + g_ref[...] * _mm(a_ref[...], w_ref)


def _residual_proj(a, w, x2, gate, seq, tm_pref, tn_pref, single_buffer_w):
    m, k = a.shape
    n = w.shape[1]
    tm = _pick(seq, tm_pref, SUBLANES)
    tn = _pick(n, tn_pref, LANES)
    tiles_per_batch = seq // tm
    w_mode = dict(pipeline_mode=pl.Buffered(1)) if single_buffer_w else {}
    return pl.pallas_call(
        _residual_kernel,
        grid=(n // tn, m // tm),
        in_specs=[
            pl.BlockSpec((tm, k), lambda j, i: (i, 0)),
            pl.BlockSpec((k, tn), lambda j, i: (0, j), **w_mode),
            pl.BlockSpec((tm, tn), lambda j, i: (i, j)),
            pl.BlockSpec((None, 1, tn), lambda j, i: (i // tiles_per_batch, 0, j)),
        ],
        out_specs=pl.BlockSpec((tm, tn), lambda j, i: (i, j)),
        out_shape=jax.ShapeDtypeStruct((m, n), F32),
        compiler_params=_params("arbitrary", "arbitrary"),
        name="residual_proj",
    )(a, w, x2, gate)


def _swiglu_kernel(a_ref, wg_ref, wu_ref, o_ref):
    a = a_ref[...]
    o_ref[...] = (_silu(_mm(a, wg_ref)) * _mm(a, wu_ref)).astype(o_ref.dtype)


def _swiglu_proj(h, w_gate, w_up):
    m, d = h.shape
    n = w_gate.shape[1]
    tm = _pick(m, 2048, SUBLANES)
    tn = _pick(n, 256, LANES)
    wspec = pl.BlockSpec((d, tn), lambda j, i: (0, j))
    return pl.pallas_call(
        _swiglu_kernel,
        grid=(n // tn, m // tm),
        in_specs=[pl.BlockSpec((tm, d), lambda j, i: (i, 0)), wspec, wspec],
        out_specs=pl.BlockSpec((tm, tn), lambda j, i: (i, j)),
        out_shape=jax.ShapeDtypeStruct((m, n), BF16),
        compiler_params=_params("arbitrary", "arbitrary"),
        name="swiglu_proj",
    )(h, w_gate, w_up)


def _attn_kernel(q_ref, k_ref, v_ref, o_ref, m_ref, l_ref, acc_ref, *, tk):
    tq = q_ref.shape[0]
    seq = k_ref.shape[0]
    kv_heads = k_ref.shape[1] // HEAD_DIM
    lane_tiles = tk // LANES
    group_w = GQA_GROUP * HEAD_DIM
    qs = [jnp.concatenate([q_ref[:, j * group_w + g * HEAD_DIM:j * group_w + (g + 1) * HEAD_DIM]
                           for g in range(GQA_GROUP)], axis=0) for j in range(kv_heads)]
    m_ref[...] = jnp.full_like(m_ref, -jnp.inf)
    l_ref[...] = jnp.zeros_like(l_ref)
    acc_ref[...] = jnp.zeros_like(acc_ref)

    def body(c, carry):
        start = pl.multiple_of(c * tk, tk)
        for j in range(kv_heads):
            cols = slice(j * HEAD_DIM, (j + 1) * HEAD_DIM)
            k = k_ref[pl.ds(start, tk), cols]
            v = v_ref[pl.ds(start, tk), cols]
            s = lax.dot_general(qs[j], k, (((1,), (1,)), ((), ())), preferred_element_type=F32)
            tiles = [s[:, t * LANES:(t + 1) * LANES] for t in range(lane_tiles)]
            tile_max = functools.reduce(jnp.maximum, tiles)
            m_prev = m_ref[j]
            m_new = jnp.maximum(m_prev, jnp.max(tile_max, axis=-1, keepdims=True))
            alpha = jnp.exp2(m_prev - m_new)
            p = [jnp.exp2(t - m_new) for t in tiles]
            l_ref[j] = alpha * l_ref[j] + functools.reduce(jnp.add, p)
            pb = jnp.concatenate([t.astype(BF16) for t in p], axis=1)
            acc_ref[j] = alpha * acc_ref[j] + jnp.dot(pb, v, preferred_element_type=F32)
            m_ref[j] = m_new
        return carry

    lax.fori_loop(0, seq // tk, body, 0, unroll=True)
    for j in range(kv_heads):
        out = acc_ref[j] / jnp.sum(l_ref[j], axis=-1, keepdims=True)
        for g in range(GQA_GROUP):
            cols = slice(j * group_w + g * HEAD_DIM, j * group_w + (g + 1) * HEAD_DIM)
            o_ref[:, cols] = out[g * tq:(g + 1) * tq].astype(o_ref.dtype)


def _attention(qkv, batch, seq, n_q, n_kv):
    m = qkv.shape[0]
    tq = _pick(seq, 256, 16)
    tk = _pick(seq, 1024, LANES)
    kv_heads = 2 if n_kv % 2 == 0 else 1
    group_w = kv_heads * GQA_GROUP * HEAD_DIM
    kv_w = kv_heads * HEAD_DIM
    q_tiles = seq // tq
    k_block0 = n_q * HEAD_DIM // kv_w
    v_block0 = (n_q + n_kv) * HEAD_DIM // kv_w
    state = pltpu.VMEM((kv_heads, GQA_GROUP * tq, LANES), F32)
    return pl.pallas_call(
        functools.partial(_attn_kernel, tk=tk),
        grid=(batch, n_kv // kv_heads, q_tiles),
        in_specs=[
            pl.BlockSpec((tq, group_w), lambda b, j, i: (b * q_tiles + i, j)),
            pl.BlockSpec((seq, kv_w), lambda b, j, i: (b, k_block0 + j)),
            pl.BlockSpec((seq, kv_w), lambda b, j, i: (b, v_block0 + j)),
        ],
        out_specs=pl.BlockSpec((tq, group_w), lambda b, j, i: (b * q_tiles + i, j)),
        out_shape=jax.ShapeDtypeStruct((m, n_q * HEAD_DIM), BF16),
        scratch_shapes=[state, state, state],
        compiler_params=_params("parallel", "parallel", "arbitrary"),
        name="attention",
    )(qkv, qkv, qkv)


def _sgu_kernel(u_ref, v_ref, gain_ref, ws_ref, bias_ref, o_ref, wsb_ref):
    @pl.when(pl.program_id(0) == 0)
    def _():
        wsb_ref[...] = ws_ref[...].astype(BF16)

    rows = u_ref.shape[0]
    groups = ws_ref.shape[0]
    for c in range(rows // CHUNK):
        rs = slice(c * CHUNK, (c + 1) * CHUNK)
        v = v_ref[rs, :].astype(F32)
        vn = v * lax.rsqrt(jnp.mean(v * v, axis=-1, keepdims=True) + EPS) * gain_ref[...]
        vn = vn.astype(BF16)
        for g in range(groups):
            cs = slice(g * LANES, (g + 1) * LANES)
            sv = jnp.dot(wsb_ref[g], vn[:, cs], preferred_element_type=F32) + bias_ref[g]
            o_ref[rs, cs] = (u_ref[rs, cs].astype(F32) * sv).astype(o_ref.dtype)


def _spatial_gate(z, v_gain, w_s, b_s):
    m, two_w = z.shape
    width = two_w // 2
    groups = w_s.shape[0]
    rows = _pick(m, 2 * CHUNK, CHUNK)
    bias = jnp.broadcast_to(b_s[:, :, None], (groups, CHUNK, LANES))
    full = lambda shape: pl.BlockSpec(shape, lambda i: (0,) * len(shape))
    return pl.pallas_call(
        _sgu_kernel,
        grid=(m // rows,),
        in_specs=[
            pl.BlockSpec((rows, width), lambda i: (i, 0)),
            pl.BlockSpec((rows, width), lambda i: (i, 1)),
            full((1, width)),
            full((groups, CHUNK, CHUNK)),
            full((groups, CHUNK, LANES)),
        ],
        out_specs=pl.BlockSpec((rows, width), lambda i: (i, 0)),
        out_shape=jax.ShapeDtypeStruct((m, width), BF16),
        scratch_shapes=[pltpu.VMEM((groups, CHUNK, CHUNK), BF16)],
        compiler_params=_params("arbitrary"),
        name="spatial_gate",
    )(z, z, v_gain.reshape(1, width), w_s, bias)


_I1, _I2, _W1, _W2, _R1, _R2 = range(6)


def _route_kernel(lg_ref, info_ref, cnt_ref, carry_ref, *, n_exp):
    @pl.when(pl.program_id(0) == 0)
    def _():
        carry_ref[...] = jnp.zeros_like(carry_ref)

    lg = lg_ref[...]
    tb = lg.shape[0]
    lane = lax.broadcasted_iota(jnp.int32, lg.shape, 1).astype(F32)
    lg = jnp.where(lane < n_exp, lg, -jnp.inf)
    v1 = jnp.max(lg, axis=-1, keepdims=True)
    i1 = jnp.min(jnp.where(lg == v1, lane, float(LANES)), axis=-1, keepdims=True)
    rest = jnp.where(lane == i1, -jnp.inf, lg)
    v2 = jnp.max(rest, axis=-1, keepdims=True)
    i2 = jnp.min(jnp.where(rest == v2, lane, float(LANES)), axis=-1, keepdims=True)
    e = jnp.exp(v2 - v1)
    w1 = 1.0 / (1.0 + e)
    w2 = e / (1.0 + e)
    sel = jnp.where(lane == i1, 1.0, jnp.where(lane == i2, 1.0, 0.0))
    row = lax.broadcasted_iota(jnp.int32, (tb, tb), 0)
    col = lax.broadcasted_iota(jnp.int32, (tb, tb), 1)
    tri = jnp.where(row > col, 1.0, 0.0).astype(BF16)
    rank = jnp.dot(tri, sel.astype(BF16), preferred_element_type=F32) + carry_ref[0:1, :]
    r1 = jnp.sum(jnp.where(lane == i1, rank, 0.0), axis=-1, keepdims=True)
    r2 = jnp.sum(jnp.where(lane == i2, rank, 0.0), axis=-1, keepdims=True)
    carry_ref[...] += jnp.sum(sel, axis=0, keepdims=True)
    info = jnp.zeros_like(lg)
    for idx, val in ((_I1, i1), (_I2, i2), (_W1, w1), (_W2, w2), (_R1, r1), (_R2, r2)):
        info = jnp.where(lane == idx, val, info)
    info_ref[...] = info
    cnt_ref[...] = carry_ref[...]


def _route(logits, n_exp):
    n = logits.shape[0]
    tb = _pick(n, 512, SUBLANES)
    return pl.pallas_call(
        functools.partial(_route_kernel, n_exp=n_exp),
        grid=(n // tb,),
        in_specs=[pl.BlockSpec((tb, LANES), lambda i: (i, 0))],
        out_specs=[pl.BlockSpec((tb, LANES), lambda i: (i, 0)),
                   pl.BlockSpec((SUBLANES, LANES), lambda i: (0, 0))],
        out_shape=[jax.ShapeDtypeStruct((n, LANES), F32),
                   jax.ShapeDtypeStruct((SUBLANES, LANES), F32)],
        scratch_shapes=[pltpu.VMEM((SUBLANES, LANES), F32)],
        compiler_params=_params("arbitrary"),
        name="route",
    )(logits)


def _row_copy_out(h_ref, xs_ref, sem, t, dst_row):
    return pltpu.make_async_copy(h_ref.at[t], xs_ref.at[dst_row], sem)


def _dispatch_kernel(pos_ref, h_ref, zero_ref, xs_ref, sem):
    del zero_ref
    tt = h_ref.shape[0]
    base = pl.program_id(0) * tt * TOP_K

    def issue(t, carry):
        for k in range(TOP_K):
            _row_copy_out(h_ref, xs_ref, sem, t, pos_ref[base + t * TOP_K + k]).start()
        return carry

    def drain(t, carry):
        for k in range(TOP_K):
            _row_copy_out(h_ref, xs_ref, sem, t, 0).wait()
        return carry

    lax.fori_loop(0, tt, issue, 0)
    lax.fori_loop(0, tt, drain, 0)


def _dispatch(h, pos, n_rows):
    n, d = h.shape
    sub = d // LANES
    tt = _pick(n, 256, SUBLANES)
    h3 = h.reshape(n, sub, LANES)
    zeros = jnp.zeros((n_rows, sub, LANES), h.dtype)
    xs = pl.pallas_call(
        _dispatch_kernel,
        grid_spec=pltpu.PrefetchScalarGridSpec(
            num_scalar_prefetch=1,
            grid=(n // tt,),
            in_specs=[pl.BlockSpec((tt, sub, LANES), lambda i, pos: (i, 0, 0)),
                      pl.BlockSpec(memory_space=pl.ANY)],
            out_specs=pl.BlockSpec(memory_space=pl.ANY),
            scratch_shapes=[pltpu.SemaphoreType.DMA(())],
        ),
        out_shape=jax.ShapeDtypeStruct((n_rows, sub, LANES), h.dtype),
        input_output_aliases={2: 0},
        compiler_params=_params("arbitrary"),
        name="dispatch",
    )(pos.reshape(-1), h3, zeros)
    return xs.reshape(n_rows, d)


(_T_EXPERT, _T_ROWS, _T_FIRST, _T_GROUP, _T_NEXT_EXPERT, _T_LAST_GROUP, _T_N_TILES,
 _T_N_GROUPS) = range(8)


def _per_tile_rows(o_ref, meta_ref, compute):
    tm = o_ref.shape[0]
    half = tm // 2
    valid = meta_ref[_T_ROWS, pl.program_id(1)]

    @pl.when(valid > half)
    def _():
        o_ref[...] = compute(slice(0, tm))

    @pl.when((valid > 0) & (valid <= half))
    def _():
        o_ref[0:half, :] = compute(slice(0, half))
        o_ref[half:tm, :] = jnp.zeros((tm - half, o_ref.shape[1]), o_ref.dtype)

    @pl.when(valid == 0)
    def _():
        o_ref[...] = jnp.zeros_like(o_ref)


def _panel_copies(w_hbm_refs, buf_refs, sems, expert, j, slot):
    tn = buf_refs[0].shape[2]
    cols = pl.ds(pl.multiple_of(j * tn, tn), tn)
    return [pltpu.make_async_copy(w.at[expert, :, cols], buf.at[slot], sems.at[slot])
            for w, buf in zip(w_hbm_refs, buf_refs)]


def _expert_panel_slot(meta_ref, w_hbm_refs, buf_refs, sems):
    j = pl.program_id(0)
    i = pl.program_id(1)
    slot = (j * meta_ref[_T_N_GROUPS, 0] + meta_ref[_T_GROUP, i]) % 2
    first = meta_ref[_T_FIRST, i] == 1

    @pl.when(first & (j == 0) & (i == 0))
    def _():
        for cp in _panel_copies(w_hbm_refs, buf_refs, sems, meta_ref[_T_EXPERT, i], j, slot):
            cp.start()

    @pl.when(first)
    def _():
        for cp in _panel_copies(w_hbm_refs, buf_refs, sems, meta_ref[_T_EXPERT, i], j, slot):
            cp.wait()
        j_next = j + meta_ref[_T_LAST_GROUP, i]

        @pl.when(j_next < pl.num_programs(0))
        def _():
            for cp in _panel_copies(w_hbm_refs, buf_refs, sems, meta_ref[_T_NEXT_EXPERT, i],
                                    j_next, 1 - slot):
                cp.start()

    return slot


def _expert_swiglu_kernel(meta_ref, a_ref, wg_hbm, wu_hbm, o_ref, wg_buf, wu_buf, sems):
    slot = _expert_panel_slot(meta_ref, (wg_hbm, wu_hbm), (wg_buf, wu_buf), sems)

    def compute(rows):
        a = a_ref[rows, :]
        return (_silu(_mm(a, wg_buf.at[slot])) * _mm(a, wu_buf.at[slot])).astype(o_ref.dtype)

    _per_tile_rows(o_ref, meta_ref, compute)


def _expert_down_kernel(meta_ref, a_ref, w_hbm, o_ref, w_buf, sems):
    slot = _expert_panel_slot(meta_ref, (w_hbm,), (w_buf,), sems)
    _per_tile_rows(o_ref, meta_ref, lambda rows: _mm(a_ref[rows, :], w_buf.at[slot]))


def _live_tile(i, meta):
    return jnp.minimum(i, meta[_T_N_TILES, 0] - 1)


def _expert_swiglu(xs, w_gate, w_up, tile_meta, tm):
    rows, d = xs.shape
    n = w_gate.shape[2]
    tn = _pick(n, 512, LANES)
    panel = pltpu.VMEM((2, d, tn), F32)
    return pl.pallas_call(
        _expert_swiglu_kernel,
        grid_spec=pltpu.PrefetchScalarGridSpec(
            num_scalar_prefetch=1,
            grid=(n // tn, rows // tm),
            in_specs=[pl.BlockSpec((tm, d), lambda j, i, meta: (_live_tile(i, meta), 0)),
                      pl.BlockSpec(memory_space=pl.ANY),
                      pl.BlockSpec(memory_space=pl.ANY)],
            out_specs=pl.BlockSpec((tm, tn), lambda j, i, meta: (i, j)),
            scratch_shapes=[panel, panel, pltpu.SemaphoreType.DMA((2,))],
        ),
        out_shape=jax.ShapeDtypeStruct((rows, n), BF16),
        compiler_params=_params("arbitrary", "arbitrary"),
        name="expert_swiglu",
    )(tile_meta, xs, w_gate, w_up)


def _expert_down(act, w_down, tile_meta, tm):
    rows, k = act.shape
    n = w_down.shape[2]
    tn = _pick(n, 1024, LANES)
    return pl.pallas_call(
        _expert_down_kernel,
        grid_spec=pltpu.PrefetchScalarGridSpec(
            num_scalar_prefetch=1,
            grid=(n // tn, rows // tm),
            in_specs=[pl.BlockSpec((tm, k), lambda j, i, meta: (_live_tile(i, meta), 0)),
                      pl.BlockSpec(memory_space=pl.ANY)],
            out_specs=pl.BlockSpec((tm, tn), lambda j, i, meta: (i, j)),
            scratch_shapes=[pltpu.VMEM((2, k, tn), F32), pltpu.SemaphoreType.DMA((2,))],
        ),
        out_shape=jax.ShapeDtypeStruct((rows, n), F32),
        compiler_params=_params("arbitrary", "arbitrary"),
        name="expert_down",
    )(tile_meta, act, w_down)


def _row_copy_in(y_ref, buf_ref, sems, slot, k, t, src_row):
    return pltpu.make_async_copy(y_ref.at[pl.ds(src_row, 1), :],
                                 buf_ref.at[slot, k, pl.ds(t, 1), :], sems.at[slot])


def _combine_kernel(pos_ref, x_ref, g_ref, info_ref, fg_ref, y_ref, o_ref, buf_ref, sems):
    tt = x_ref.shape[0]
    i = pl.program_id(0)
    slot = i % 2

    def start_gather(step, into):
        base = step * tt * TOP_K

        def issue(t, carry):
            for k in range(TOP_K):
                _row_copy_in(y_ref, buf_ref, sems, into, k, t,
                             pos_ref[base + t * TOP_K + k]).start()
            return carry

        lax.fori_loop(0, tt, issue, 0)

    @pl.when(i == 0)
    def _():
        start_gather(0, 0)

    @pl.when(i + 1 < pl.num_programs(0))
    def _():
        start_gather(i + 1, 1 - slot)

    def drain(t, carry):
        for k in range(TOP_K):
            _row_copy_in(y_ref, buf_ref, sems, slot, k, t, 0).wait()
        return carry

    lax.fori_loop(0, tt, drain, 0)
    info = info_ref[...]
    f = info[:, _W1:_W1 + 1] * buf_ref[slot, 0] + info[:, _W2:_W2 + 1] * buf_ref[slot, 1]
    xn = x_ref[...] + g_ref[...] * f
    o_ref[...] = xn * lax.rsqrt(jnp.mean(xn * xn, axis=-1, keepdims=True) + EPS) * fg_ref[...]


def _combine_norm(x2, gate, info, pos, y, final_g, seq):
    n, d = x2.shape
    tt = _pick(seq, 128, SUBLANES)
    tiles_per_batch = seq // tt
    return pl.pallas_call(
        _combine_kernel,
        grid_spec=pltpu.PrefetchScalarGridSpec(
            num_scalar_prefetch=1,
            grid=(n // tt,),
            in_specs=[pl.BlockSpec((tt, d), lambda i, pos: (i, 0)),
                      pl.BlockSpec((None, 1, d), lambda i, pos: (i // tiles_per_batch, 0, 0)),
                      pl.BlockSpec((tt, LANES), lambda i, pos: (i, 0)),
                      pl.BlockSpec((1, d), lambda i, pos: (0, 0)),
                      pl.BlockSpec(memory_space=pl.ANY)],
            out_specs=pl.BlockSpec((tt, d), lambda i, pos: (i, 0)),
            scratch_shapes=[pltpu.VMEM((2, TOP_K, tt, d), F32),
                            pltpu.SemaphoreType.DMA((2,))],
        ),
        out_shape=jax.ShapeDtypeStruct((n, d), F32),
        compiler_params=_params("arbitrary"),
        name="combine_norm",
    )(pos.reshape(-1), x2, gate, info, final_g.reshape(1, d), y)


def _moe_layer(x2, g, shift, scale, gate, w_router, w_gate, w_up, w_down, final_g, seq):
    n = x2.shape[0]
    n_exp = w_router.shape[1]
    h, logits = _modulate_router(x2, g, shift, scale, w_router, seq)
    info, cnt = _route(logits, n_exp)

    tm = _pick(n, 512, SUBLANES)
    max_tiles = n * TOP_K // tm + n_exp
    counts = cnt[0, :n_exp].astype(jnp.int32)
    tiles_per_exp = (counts + tm - 1) // tm
    tile_end = jnp.cumsum(tiles_per_exp)
    offsets = (tile_end - tiles_per_exp) * tm
    n_tiles = tile_end[-1:]
    tile_ids = jnp.arange(max_tiles, dtype=jnp.int32)
    tile_expert = jnp.minimum(
        jnp.sum((tile_ids[:, None] >= tile_end[None, :]).astype(jnp.int32), axis=1), n_exp - 1)
    live = tile_ids < n_tiles[0]
    first_tile = (tile_end - tiles_per_exp)[tile_expert]
    tile_rows = jnp.where(live, jnp.clip(counts[tile_expert] - (tile_ids - first_tile) * tm, 0, tm), 0)
    first = live & (tile_ids == first_tile)
    group = jnp.cumsum(first.astype(jnp.int32)) - 1
    has_tiles = tiles_per_exp > 0
    experts = jnp.arange(n_exp, dtype=jnp.int32)
    later = (experts[:, None] + 1 + experts[None, :]) % n_exp
    steps_ahead = jnp.min(jnp.where(has_tiles[later], experts[None, :], n_exp), axis=1)
    next_expert = jnp.take_along_axis(later, jnp.minimum(steps_ahead, n_exp - 1)[:, None], axis=1)[:, 0]
    last_expert = jnp.max(jnp.where(has_tiles, experts, 0))
    tile_meta = jnp.stack([
        tile_expert,
        tile_rows,
        first.astype(jnp.int32),
        group,
        next_expert[tile_expert],
        (tile_expert == last_expert).astype(jnp.int32),
        jnp.broadcast_to(n_tiles, (max_tiles,)),
        jnp.broadcast_to(jnp.sum(first.astype(jnp.int32)), (max_tiles,)),
    ]).astype(jnp.int32)
    idx = info[:, _I1:_I2 + 1].astype(jnp.int32)
    pos = offsets[idx] + info[:, _R1:_R2 + 1].astype(jnp.int32)

    xs = _dispatch(h, pos, max_tiles * tm)
    act = _expert_swiglu(xs, w_gate, w_up, tile_meta, tm)
    y = _expert_down(act, w_down, tile_meta, tm)
    return _combine_norm(x2, gate, info, pos, y, final_g, seq)


def _rope_tables(seq):
    n_freq = HEAD_DIM // 4
    t = jnp.arange(seq, dtype=jnp.int32)
    inv_freq = 1.0 / (ROPE_THETA ** (jnp.arange(n_freq, dtype=F32) * 2.0 / (HEAD_DIM // 2)))
    row = (t // GRID_W).astype(F32)[:, None] * inv_freq
    col = (t % GRID_W).astype(F32)[:, None] * inv_freq
    cos = jnp.concatenate([jnp.cos(row), jnp.cos(row), jnp.cos(col), jnp.cos(col)], axis=1)
    sin = jnp.concatenate([-jnp.sin(row), jnp.sin(row), -jnp.sin(col), jnp.sin(col)], axis=1)
    return cos, sin


def kernel(x, c, w_mod, b_mod, norm_g, final_g, attn_w_qkv, attn_q_gain, attn_k_gain, attn_w_o, mix_w_uv, mix_v_gain, mix_w_s, mix_b_s, mix_w_out, ffn_w_gate, ffn_w_up, ffn_w_down, moe_w_router, moe_w_gate, moe_w_up, moe_w_down):
    batch, seq, d = x.shape
    depth = w_mod.shape[0]
    assert depth == 2 and d % (GQA_GROUP * HEAD_DIM) == 0 and seq % CHUNK == 0
    n_q = d // HEAD_DIM
    n_kv = n_q // GQA_GROUP
    cos, sin = _rope_tables(seq)

    mod = _adaln_mod(c, w_mod, b_mod).reshape(depth, batch, 6, 1, d)
    vec = lambda layer, which: mod[layer, :, which]
    x2 = x.reshape(batch * seq, d)

    h = _modulate(x2, norm_g[0, 0], vec(0, 0), vec(0, 1), seq)
    qkv = _qkv_proj(h, attn_w_qkv[0], attn_q_gain[0], attn_k_gain[0], cos, sin, seq, n_q, n_kv)
    o = _attention(qkv, batch, seq, n_q, n_kv)
    x2 = _residual_proj(o, attn_w_o[0], x2, vec(0, 2), seq, 512, 1024, False)
    h = _modulate(x2, norm_g[0, 1], vec(0, 3), vec(0, 4), seq)
    act = _swiglu_proj(h, ffn_w_gate[0], ffn_w_up[0])
    x2 = _residual_proj(act, ffn_w_down[0], x2, vec(0, 5), seq, 512, 512, True)

    h = _modulate(x2, norm_g[1, 0], vec(1, 0), vec(1, 1), seq)
    z = _gelu_proj(h, mix_w_uv[0])
    gated = _spatial_gate(z, mix_v_gain[0], mix_w_s[0], mix_b_s[0])
    x2 = _residual_proj(gated, mix_w_out[0], x2, vec(1, 2), seq, 512, 1024, False)
    out = _moe_layer(x2, norm_g[1, 1], vec(1, 3), vec(1, 4), vec(1, 5), moe_w_router[0],
                     moe_w_gate[0], moe_w_up[0], moe_w_down[0], final_g, seq)
    return out.reshape(batch, seq, d)
```

```python
import functools

import jax
import jax.numpy as jnp
from jax import lax
from jax.experimental import pallas as pl
from jax.experimental.pallas import tpu as pltpu

EPS = 1e-6
HEAD_DIM = 128
GQA_GROUP = 4
GRID_W = 64
ROPE_THETA = 10000.0
CHUNK = 128
TOP_K = 2
LOG2_E = 1.4426950408889634
LANES = 128
SUBLANES = 8
QKV_ROW_SPLITS = 4
VMEM_LIMIT_BYTES = 56 * 1024 * 1024

F32 = jnp.float32
BF16 = jnp.bfloat16


def _pick(dim, pref, mult):
    t = max(min(pref, dim) // mult * mult, mult)
    while t > mult and dim % t:
        t -= mult
    assert dim % t == 0, (dim, pref, mult)
    return t


def _params(*sem):
    return pltpu.CompilerParams(dimension_semantics=sem, vmem_limit_bytes=VMEM_LIMIT_BYTES)


def _silu(v):
    return v * jax.nn.sigmoid(v)


def _matvec_partial(cond_ref, w_ref, b, rows=slice(None)):
    tn = w_ref.shape[1]
    cb = cond_ref[b, rows, :]
    parts = []
    for j in range(tn // LANES):
        prod = w_ref[rows, j * LANES:(j + 1) * LANES] * cb
        parts.append(prod.reshape(prod.shape[0] // SUBLANES, SUBLANES, LANES).sum(axis=0))
    return jnp.concatenate(parts, axis=1)


def _cond_kernel(c_ref, o_ref):
    o_ref[...] = _silu(c_ref[...])


def _cond_lanes(c):
    nb, d = c.shape
    tk = _pick(d, 512, SUBLANES)
    spec = pl.BlockSpec((nb, tk, LANES), lambda k: (0, k, 0))
    return pl.pallas_call(
        _cond_kernel,
        grid=(d // tk,),
        in_specs=[spec],
        out_specs=spec,
        out_shape=jax.ShapeDtypeStruct((nb, d, LANES), F32),
        compiler_params=_params("parallel"),
        name="cond_lanes",
    )(jnp.broadcast_to(c[:, :, None], (nb, d, LANES)))


def _mod_kernel(cond_ref, w_ref, b_ref, o_ref, acc_ref):
    k = pl.program_id(2)
    nb = cond_ref.shape[0]

    @pl.when(k == 0)
    def _():
        acc_ref[...] = jnp.zeros_like(acc_ref)

    for b in range(nb):
        acc_ref[b] += _matvec_partial(cond_ref, w_ref, b)

    @pl.when(k == pl.num_programs(2) - 1)
    def _():
        for b in range(nb):
            o_ref[b:b + 1, :] = acc_ref[b].sum(axis=0, keepdims=True) + b_ref[...]


def _adaln_mod(cond, w_mod, b_mod, depth):
    _, d, n = w_mod.shape
    nb = cond.shape[0]
    tk = _pick(d, 512, SUBLANES)
    tn = _pick(n, 2048, LANES)
    return pl.pallas_call(
        _mod_kernel,
        grid=(depth, n // tn, d // tk),
        in_specs=[
            pl.BlockSpec((nb, tk, LANES), lambda l, j, k: (0, k, 0)),
            pl.BlockSpec((None, tk, tn), lambda l, j, k: (l, k, j)),
            pl.BlockSpec((None, 1, tn), lambda l, j, k: (l, 0, j)),
        ],
        out_specs=pl.BlockSpec((None, nb, tn), lambda l, j, k: (l, 0, j)),
        out_shape=jax.ShapeDtypeStruct((depth, nb, n), F32),
        scratch_shapes=[pltpu.VMEM((nb, SUBLANES, tn), F32)],
        compiler_params=_params("arbitrary", "arbitrary", "arbitrary"),
        name="adaln_mod",
    )(cond, w_mod, b_mod.reshape(-1, 1, n))


def _modulated(x_ref, g_ref, sh_ref, sc_ref):
    x = x_ref[...]
    y = x * lax.rsqrt(jnp.mean(x * x, axis=-1, keepdims=True) + EPS) * g_ref[...]
    return y * (1.0 + sc_ref[...]) + sh_ref[...]


def _modulate_kernel(x_ref, g_ref, sh_ref, sc_ref, o_ref):
    o_ref[...] = _modulated(x_ref, g_ref, sh_ref, sc_ref).astype(o_ref.dtype)


def _modulate_router_kernel(x_ref, g_ref, sh_ref, sc_ref, whi_ref, wlo_ref, o_ref, lg_ref):
    h = _modulated(x_ref, g_ref, sh_ref, sc_ref)
    hb = h.astype(BF16)
    o_ref[...] = hb
    h_lo = (h - hb.astype(F32)).astype(BF16)
    lg = jnp.dot(hb, whi_ref[...], preferred_element_type=F32)
    lg += jnp.dot(h_lo, whi_ref[...], preferred_element_type=F32)
    lg += jnp.dot(hb, wlo_ref[...], preferred_element_type=F32)
    lg_ref[...] = lg


def _row_specs(d, tm, tiles_per_batch):
    vec = pl.BlockSpec((None, 1, d), lambda i: (i // tiles_per_batch, 0, 0))
    return [
        pl.BlockSpec((tm, d), lambda i: (i, 0)),
        pl.BlockSpec((1, d), lambda i: (0, 0)),
        vec,
        vec,
    ]


def _modulate(x2, g, shift, scale, seq):
    m, d = x2.shape
    tm = _pick(seq, 256, SUBLANES)
    return pl.pallas_call(
        _modulate_kernel,
        grid=(m // tm,),
        in_specs=_row_specs(d, tm, seq // tm),
        out_specs=pl.BlockSpec((tm, d), lambda i: (i, 0)),
        out_shape=jax.ShapeDtypeStruct((m, d), BF16),
        compiler_params=_params("parallel"),
        name="modulate",
    )(x2, g.reshape(1, d), shift, scale)


def _modulate_router(x2, g, shift, scale, w_router, seq):
    m, d = x2.shape
    n_exp = w_router.shape[1]
    tm = _pick(seq, 256, SUBLANES)
    w_pad = jnp.zeros((d, LANES), F32).at[:, :n_exp].set(w_router)
    w_hi = w_pad.astype(BF16)
    w_lo = (w_pad - w_hi.astype(F32)).astype(BF16)
    wspec = pl.BlockSpec((d, LANES), lambda i: (0, 0))
    return pl.pallas_call(
        _modulate_router_kernel,
        grid=(m // tm,),
        in_specs=_row_specs(d, tm, seq // tm) + [wspec, wspec],
        out_specs=[pl.BlockSpec((tm, d), lambda i: (i, 0)),
                   pl.BlockSpec((tm, LANES), lambda i: (i, 0))],
        out_shape=[jax.ShapeDtypeStruct((m, d), BF16),
                   jax.ShapeDtypeStruct((m, LANES), F32)],
        compiler_params=_params("parallel"),
        name="modulate_router",
    )(x2, g.reshape(1, d), shift, scale, w_hi, w_lo)


def _mm(a, w_ref):
    return jnp.dot(a, w_ref[...].astype(BF16), preferred_element_type=F32)


def _rope_rotate(v, first_half):
    return jnp.where(first_half, pltpu.roll(v, LANES - 32, 1), pltpu.roll(v, 32, 1))


def _qkv_kernel(a_ref, w_ref, gain_ref, cos_ref, sin_ref, cond_ref, wm_ref, bm_ref,
                o_ref, mod_ref):
    nb = cond_ref.shape[0]
    k_slab = wm_ref.shape[0] // QKV_ROW_SPLITS
    mod_acc = [0.0] * nb
    half = a_ref.shape[0] // QKV_ROW_SPLITS
    lane = lax.broadcasted_iota(jnp.int32, (half, HEAD_DIM), 1)
    first_half = (lane % 64) < 32
    for r in range(QKV_ROW_SPLITS):
        rows = slice(r * half, (r + 1) * half)
        for b in range(nb):
            mod_acc[b] += _matvec_partial(cond_ref, wm_ref, b, slice(r * k_slab, (r + 1) * k_slab))
        acc = _mm(a_ref[rows, :], w_ref)
        cos = cos_ref[rows, :]
        sin = sin_ref[rows, :]
        for h in range(acc.shape[1] // HEAD_DIM):
            sl = slice(h * HEAD_DIM, (h + 1) * HEAD_DIM)
            v = acc[:, sl]
            norm = lax.rsqrt(jnp.mean(v * v, axis=-1, keepdims=True) + EPS) * gain_ref[0:1, sl]
            v = v * jnp.where(gain_ref[1:2, sl] > 0.0, norm, 1.0)
            o_ref[rows, sl] = (v * cos + _rope_rotate(v, first_half) * sin).astype(o_ref.dtype)
    for b in range(nb):
        mod_ref[b:b + 1, :] = mod_acc[b].sum(axis=0, keepdims=True) + bm_ref[...]


def _qkv_proj(h, w_qkv, q_gain, k_gain, cos, sin, seq, n_q, n_kv, cond, w_mod, b_mod, layer):
    m, d = h.shape
    n = w_qkv.shape[1]
    nb = cond.shape[0]
    n_mod = w_mod.shape[2]
    tm = _pick(seq, 1024, SUBLANES)
    tn = _pick(n_kv * HEAD_DIM, 512, LANES)
    heads_per_tile = tn // HEAD_DIM
    n_q_tiles = n_q * HEAD_DIM // tn
    n_k_tiles = n_kv * HEAD_DIM // tn
    n_rope_tiles = n_q_tiles + n_k_tiles
    qg = jnp.tile(q_gain * (HEAD_DIM ** -0.5 * LOG2_E), heads_per_tile)
    kg = jnp.tile(k_gain, heads_per_tile)
    gains = jnp.concatenate([jnp.broadcast_to(qg, (n_q_tiles, tn)),
                             jnp.broadcast_to(kg, (n_k_tiles, tn)),
                             jnp.ones((n_k_tiles, tn), F32)])
    normed = jnp.broadcast_to((jnp.arange(n // tn) < n_rope_tiles).astype(F32)[:, None],
                              (n // tn, tn))
    gains = jnp.stack([gains, normed], axis=1)
    cos2 = jnp.stack([cos, jnp.ones_like(cos)])
    sin2 = jnp.stack([sin, jnp.zeros_like(sin)])
    tiles_per_seq = seq // tm
    tab = pl.BlockSpec((None, tm, HEAD_DIM),
                       lambda j, i: ((j >= n_rope_tiles).astype(jnp.int32), i % tiles_per_seq, 0))
    grid = (n // tn, m // tm)
    steps = grid[0] * grid[1]
    tnm = next(t for t in range(LANES, n_mod + 1, LANES) if n_mod % t == 0 and n_mod // t <= steps)

    def chunk(j, i):
        return jnp.minimum(j * grid[1] + i, n_mod // tnm - 1)

    return pl.pallas_call(
        _qkv_kernel,
        grid=grid,
        in_specs=[
            pl.BlockSpec((tm, d), lambda j, i: (i, 0)),
            pl.BlockSpec((d, tn), lambda j, i: (0, j)),
            pl.BlockSpec((None, 2, tn), lambda j, i: (j, 0, 0)),
            tab,
            tab,
            pl.BlockSpec((nb, d, LANES), lambda j, i: (0, 0, 0)),
            pl.BlockSpec((None, d, tnm), lambda j, i: (layer, 0, chunk(j, i))),
            pl.BlockSpec((None, 1, tnm), lambda j, i: (layer, 0, chunk(j, i))),
        ],
        out_specs=[pl.BlockSpec((tm, tn), lambda j, i: (i, j)),
                   pl.BlockSpec((nb, tnm), lambda j, i: (0, chunk(j, i)))],
        out_shape=[jax.ShapeDtypeStruct((m, n), BF16),
                   jax.ShapeDtypeStruct((nb, n_mod), F32)],
        compiler_params=_params("arbitrary", "arbitrary"),
        name="qkv_proj",
    )(h, w_qkv, gains, cos2, sin2, cond, w_mod, b_mod.reshape(-1, 1, n_mod))


def _gelu_kernel(a_ref, w_ref, o_ref):
    acc = _mm(a_ref[...], w_ref)
    o_ref[...] = (0.5 * acc * (1.0 + lax.erf(acc * (2.0 ** -0.5)))).astype(o_ref.dtype)


def _gelu_proj(h, w):
    m, d = h.shape
    n = w.shape[1]
    tm = _pick(m, 512, SUBLANES)
    tn = _pick(n, 1024, LANES)
    return pl.pallas_call(
        _gelu_kernel,
        grid=(n // tn, m // tm),
        in_specs=[pl.BlockSpec((tm, d), lambda j, i: (i, 0)),
                  pl.BlockSpec((d, tn), lambda j, i: (0, j))],
        out_specs=pl.BlockSpec((tm, tn), lambda j, i: (i, j)),
        out_shape=jax.ShapeDtypeStruct((m, n), BF16),
        compiler_params=_params("arbitrary", "arbitrary"),
        name="gelu_proj",
    )(h, w)


def _residual_kernel(a_ref, w_ref, x_ref, g_ref, o_ref):
    o_ref[...] = x_ref[...] + g_ref[...] * _mm(a_ref[...], w_ref)


def _residual_proj(a, w, x2, gate, seq, tm_pref, tn_pref, single_buffer_w):
    m, k = a.shape
    n = w.shape[1]
    tm = _pick(seq, tm_pref, SUBLANES)
    tn = _pick(n, tn_pref, LANES)
    tiles_per_batch = seq // tm
    w_mode = dict(pipeline_mode=pl.Buffered(1)) if single_buffer_w else {}
    return pl.pallas_call(
        _residual_kernel,
        grid=(n // tn, m // tm),
        in_specs=[
            pl.BlockSpec((tm, k), lambda j, i: (i, 0)),
            pl.BlockSpec((k, tn), lambda j, i: (0, j), **w_mode),
            pl.BlockSpec((tm, tn), lambda j, i: (i, j)),
            pl.BlockSpec((None, 1, tn), lambda j, i: (i // tiles_per_batch, 0, j)),
        ],
        out_specs=pl.BlockSpec((tm, tn), lambda j, i: (i, j)),
        out_shape=jax.ShapeDtypeStruct((m, n), F32),
        compiler_params=_params("arbitrary", "arbitrary"),
        name="residual_proj",
    )(a, w, x2, gate)


def _swiglu_kernel(a_ref, wg_ref, wu_ref, o_ref):
    a = a_ref[...]
    o_ref[...] = (_silu(_mm(a, wg_ref)) * _mm(a, wu_ref)).astype(o_ref.dtype)


def _swiglu_proj(h, w_gate, w_up):
    m, d = h.shape
    n = w_gate.shape[1]
    tm = _pick(m, 2048, SUBLANES)
    tn = _pick(n, 256, LANES)
    wspec = pl.BlockSpec((d, tn), lambda j, i: (0, j))
    return pl.pallas_call(
        _swiglu_kernel,
        grid=(n // tn, m // tm),
        in_specs=[pl.BlockSpec((tm, d), lambda j, i: (i, 0)), wspec, wspec],
        out_specs=pl.BlockSpec((tm, tn), lambda j, i: (i, j)),
        out_shape=jax.ShapeDtypeStruct((m, n), BF16),
        compiler_params=_params("arbitrary", "arbitrary"),
        name="swiglu_proj",
    )(h, w_gate, w_up)


def _attn_kernel(q_ref, k_ref, v_ref, o_ref, m_ref, l_ref, acc_ref, *, tk):
    tq = q_ref.shape[0]
    seq = k_ref.shape[0]
    kv_heads = k_ref.shape[1] // HEAD_DIM
    lane_tiles = tk // LANES
    group_w = GQA_GROUP * HEAD_DIM
    qs = [jnp.concatenate([q_ref[:, j * group_w + g * HEAD_DIM:j * group_w + (g + 1) * HEAD_DIM]
                           for g in range(GQA_GROUP)], axis=0) for j in range(kv_heads)]
    m_ref[...] = jnp.full_like(m_ref, -jnp.inf)
    l_ref[...] = jnp.zeros_like(l_ref)
    acc_ref[...] = jnp.zeros_like(acc_ref)

    def body(c, carry):
        start = pl.multiple_of(c * tk, tk)
        for j in range(kv_heads):
            cols = slice(j * HEAD_DIM, (j + 1) * HEAD_DIM)
            k = k_ref[pl.ds(start, tk), cols]
            v = v_ref[pl.ds(start, tk), cols]
            s = lax.dot_general(qs[j], k, (((1,), (1,)), ((), ())), preferred_element_type=F32)
            tiles = [s[:, t * LANES:(t + 1) * LANES] for t in range(lane_tiles)]
            tile_max = functools.reduce(jnp.maximum, tiles)
            m_prev = m_ref[j]
            m_new = jnp.maximum(m_prev, jnp.max(tile_max, axis=-1, keepdims=True))
            alpha = jnp.exp2(m_prev - m_new)
            p = [jnp.exp2(t - m_new) for t in tiles]
            l_ref[j] = alpha * l_ref[j] + functools.reduce(jnp.add, p)
            pb = jnp.concatenate([t.astype(BF16) for t in p], axis=1)
            acc_ref[j] = alpha * acc_ref[j] + jnp.dot(pb, v, preferred_element_type=F32)
            m_ref[j] = m_new
        return carry

    lax.fori_loop(0, seq // tk, body, 0, unroll=True)
    for j in range(kv_heads):
        out = acc_ref[j] / jnp.sum(l_ref[j], axis=-1, keepdims=True)
        for g in range(GQA_GROUP):
            cols = slice(j * group_w + g * HEAD_DIM, j * group_w + (g + 1) * HEAD_DIM)
            o_ref[:, cols] = out[g * tq:(g + 1) * tq].astype(o_ref.dtype)


def _attention(qkv, batch, seq, n_q, n_kv):
    m = qkv.shape[0]
    tq = _pick(seq, 256, 16)
    tk = _pick(seq, 1024, LANES)
    kv_heads = 2 if n_kv % 2 == 0 else 1
    group_w = kv_heads * GQA_GROUP * HEAD_DIM
    kv_w = kv_heads * HEAD_DIM
    q_tiles = seq // tq
    k_block0 = n_q * HEAD_DIM // kv_w
    v_block0 = (n_q + n_kv) * HEAD_DIM // kv_w
    state = pltpu.VMEM((kv_heads, GQA_GROUP * tq, LANES), F32)
    return pl.pallas_call(
        functools.partial(_attn_kernel, tk=tk),
        grid=(batch, n_kv // kv_heads, q_tiles),
        in_specs=[
            pl.BlockSpec((tq, group_w), lambda b, j, i: (b * q_tiles + i, j)),
            pl.BlockSpec((seq, kv_w), lambda b, j, i: (b, k_block0 + j)),
            pl.BlockSpec((seq, kv_w), lambda b, j, i: (b, v_block0 + j)),
        ],
        out_specs=pl.BlockSpec((tq, group_w), lambda b, j, i: (b * q_tiles + i, j)),
        out_shape=jax.ShapeDtypeStruct((m, n_q * HEAD_DIM), BF16),
        scratch_shapes=[state, state, state],
        compiler_params=_params("parallel", "parallel", "arbitrary"),
        name="attention",
    )(qkv, qkv, qkv)


def _sgu_kernel(u_ref, v_ref, gain_ref, ws_ref, bias_ref, o_ref, wsb_ref):
    @pl.when(pl.program_id(0) == 0)
    def _():
        wsb_ref[...] = ws_ref[...].astype(BF16)

    rows = u_ref.shape[0]
    groups = ws_ref.shape[0]
    for c in range(rows // CHUNK):
        rs = slice(c * CHUNK, (c + 1) * CHUNK)
        v = v_ref[rs, :].astype(F32)
        vn = v * lax.rsqrt(jnp.mean(v * v, axis=-1, keepdims=True) + EPS) * gain_ref[...]
        vn = vn.astype(BF16)
        for g in range(groups):
            cs = slice(g * LANES, (g + 1) * LANES)
            sv = jnp.dot(wsb_ref[g], vn[:, cs], preferred_element_type=F32) + bias_ref[g]
            o_ref[rs, cs] = (u_ref[rs, cs].astype(F32) * sv).astype(o_ref.dtype)


def _spatial_gate(z, v_gain, w_s, b_s):
    m, two_w = z.shape
    width = two_w // 2
    groups = w_s.shape[0]
    rows = _pick(m, 2 * CHUNK, CHUNK)
    bias = jnp.broadcast_to(b_s[:, :, None], (groups, CHUNK, LANES))
    full = lambda shape: pl.BlockSpec(shape, lambda i: (0,) * len(shape))
    return pl.pallas_call(
        _sgu_kernel,
        grid=(m // rows,),
        in_specs=[
            pl.BlockSpec((rows, width), lambda i: (i, 0)),
            pl.BlockSpec((rows, width), lambda i: (i, 1)),
            full((1, width)),
            full((groups, CHUNK, CHUNK)),
            full((groups, CHUNK, LANES)),
        ],
        out_specs=pl.BlockSpec((rows, width), lambda i: (i, 0)),
        out_shape=jax.ShapeDtypeStruct((m, width), BF16),
        scratch_shapes=[pltpu.VMEM((groups, CHUNK, CHUNK), BF16)],
        compiler_params=_params("arbitrary"),
        name="spatial_gate",
    )(z, z, v_gain.reshape(1, width), w_s, bias)


_I1, _I2, _W1, _W2, _R1, _R2 = range(6)


def _route_kernel(lg_ref, info_ref, cnt_ref, carry_ref, *, n_exp):
    @pl.when(pl.program_id(0) == 0)
    def _():
        carry_ref[...] = jnp.zeros_like(carry_ref)

    lg = lg_ref[...]
    tb = lg.shape[0]
    lane = lax.broadcasted_iota(jnp.int32, lg.shape, 1).astype(F32)
    lg = jnp.where(lane < n_exp, lg, -jnp.inf)
    v1 = jnp.max(lg, axis=-1, keepdims=True)
    i1 = jnp.min(jnp.where(lg == v1, lane, float(LANES)), axis=-1, keepdims=True)
    rest = jnp.where(lane == i1, -jnp.inf, lg)
    v2 = jnp.max(rest, axis=-1, keepdims=True)
    i2 = jnp.min(jnp.where(rest == v2, lane, float(LANES)), axis=-1, keepdims=True)
    e = jnp.exp(v2 - v1)
    w1 = 1.0 / (1.0 + e)
    w2 = e / (1.0 + e)
    sel = jnp.where(lane == i1, 1.0, jnp.where(lane == i2, 1.0, 0.0))
    row = lax.broadcasted_iota(jnp.int32, (tb, tb), 0)
    col = lax.broadcasted_iota(jnp.int32, (tb, tb), 1)
    tri = jnp.where(row > col, 1.0, 0.0).astype(BF16)
    rank = jnp.dot(tri, sel.astype(BF16), preferred_element_type=F32) + carry_ref[0:1, :]
    r1 = jnp.sum(jnp.where(lane == i1, rank, 0.0), axis=-1, keepdims=True)
    r2 = jnp.sum(jnp.where(lane == i2, rank, 0.0), axis=-1, keepdims=True)
    carry_ref[...] += jnp.sum(sel, axis=0, keepdims=True)
    info = jnp.zeros_like(lg)
    for idx, val in ((_I1, i1), (_I2, i2), (_W1, w1), (_W2, w2), (_R1, r1), (_R2, r2)):
        info = jnp.where(lane == idx, val, info)
    info_ref[...] = info
    cnt_ref[...] = carry_ref[...]


def _route(logits, n_exp):
    n = logits.shape[0]
    tb = _pick(n, 512, SUBLANES)
    return pl.pallas_call(
        functools.partial(_route_kernel, n_exp=n_exp),
        grid=(n // tb,),
        in_specs=[pl.BlockSpec((tb, LANES), lambda i: (i, 0))],
        out_specs=[pl.BlockSpec((tb, LANES), lambda i: (i, 0)),
                   pl.BlockSpec((SUBLANES, LANES), lambda i: (0, 0))],
        out_shape=[jax.ShapeDtypeStruct((n, LANES), F32),
                   jax.ShapeDtypeStruct((SUBLANES, LANES), F32)],
        scratch_shapes=[pltpu.VMEM((SUBLANES, LANES), F32)],
        compiler_params=_params("arbitrary"),
        name="route",
    )(logits)


def _row_copy_out(h_ref, xs_ref, sem, t, dst_row):
    return pltpu.make_async_copy(h_ref.at[t], xs_ref.at[dst_row], sem)


def _dispatch_kernel(pos_ref, h_ref, zero_ref, xs_ref, sem):
    del zero_ref
    tt = h_ref.shape[0]
    base = pl.program_id(0) * tt * TOP_K

    def issue(t, carry):
        for k in range(TOP_K):
            _row_copy_out(h_ref, xs_ref, sem, t, pos_ref[base + t * TOP_K + k]).start()
        return carry

    def drain(t, carry):
        for k in range(TOP_K):
            _row_copy_out(h_ref, xs_ref, sem, t, 0).wait()
        return carry

    lax.fori_loop(0, tt, issue, 0)
    lax.fori_loop(0, tt, drain, 0)


def _dispatch(h, pos, n_rows):
    n, d = h.shape
    sub = d // LANES
    tt = _pick(n, 256, SUBLANES)
    h3 = h.reshape(n, sub, LANES)
    zeros = jnp.zeros((n_rows, sub, LANES), h.dtype)
    xs = pl.pallas_call(
        _dispatch_kernel,
        grid_spec=pltpu.PrefetchScalarGridSpec(
            num_scalar_prefetch=1,
            grid=(n // tt,),
            in_specs=[pl.BlockSpec((tt, sub, LANES), lambda i, pos: (i, 0, 0)),
                      pl.BlockSpec(memory_space=pl.ANY)],
            out_specs=pl.BlockSpec(memory_space=pl.ANY),
            scratch_shapes=[pltpu.SemaphoreType.DMA(())],
        ),
        out_shape=jax.ShapeDtypeStruct((n_rows, sub, LANES), h.dtype),
        input_output_aliases={2: 0},
        compiler_params=_params("arbitrary"),
        name="dispatch",
    )(pos.reshape(-1), h3, zeros)
    return xs.reshape(n_rows, d)


(_T_EXPERT, _T_ROWS, _T_FIRST, _T_GROUP, _T_NEXT_EXPERT, _T_LAST_GROUP, _T_N_TILES,
 _T_N_GROUPS) = range(8)


def _per_tile_rows(o_ref, meta_ref, compute):
    tm = o_ref.shape[0]
    half = tm // 2
    valid = meta_ref[_T_ROWS, pl.program_id(1)]

    @pl.when(valid > half)
    def _():
        o_ref[...] = compute(slice(0, tm))

    @pl.when((valid > 0) & (valid <= half))
    def _():
        o_ref[0:half, :] = compute(slice(0, half))
        o_ref[half:tm, :] = jnp.zeros((tm - half, o_ref.shape[1]), o_ref.dtype)

    @pl.when(valid == 0)
    def _():
        o_ref[...] = jnp.zeros_like(o_ref)


def _panel_copies(w_hbm_refs, buf_refs, sems, expert, j, slot):
    tn = buf_refs[0].shape[2]
    cols = pl.ds(pl.multiple_of(j * tn, tn), tn)
    return [pltpu.make_async_copy(w.at[expert, :, cols], buf.at[slot], sems.at[slot])
            for w, buf in zip(w_hbm_refs, buf_refs)]


def _expert_panel_slot(meta_ref, w_hbm_refs, buf_refs, sems):
    j = pl.program_id(0)
    i = pl.program_id(1)
    slot = (j * meta_ref[_T_N_GROUPS, 0] + meta_ref[_T_GROUP, i]) % 2
    first = meta_ref[_T_FIRST, i] == 1

    @pl.when(first & (j == 0) & (i == 0))
    def _():
        for cp in _panel_copies(w_hbm_refs, buf_refs, sems, meta_ref[_T_EXPERT, i], j, slot):
            cp.start()

    @pl.when(first)
    def _():
        for cp in _panel_copies(w_hbm_refs, buf_refs, sems, meta_ref[_T_EXPERT, i], j, slot):
            cp.wait()
        j_next = j + meta_ref[_T_LAST_GROUP, i]

        @pl.when(j_next < pl.num_programs(0))
        def _():
            for cp in _panel_copies(w_hbm_refs, buf_refs, sems, meta_ref[_T_NEXT_EXPERT, i],
                                    j_next, 1 - slot):
                cp.start()

    return slot


def _expert_swiglu_kernel(meta_ref, a_ref, wg_hbm, wu_hbm, o_ref, wg_buf, wu_buf, sems):
    slot = _expert_panel_slot(meta_ref, (wg_hbm, wu_hbm), (wg_buf, wu_buf), sems)

    def compute(rows):
        a = a_ref[rows, :]
        return (_silu(_mm(a, wg_buf.at[slot])) * _mm(a, wu_buf.at[slot])).astype(o_ref.dtype)

    _per_tile_rows(o_ref, meta_ref, compute)


def _expert_down_kernel(meta_ref, a_ref, w_hbm, o_ref, w_buf, sems):
    slot = _expert_panel_slot(meta_ref, (w_hbm,), (w_buf,), sems)
    _per_tile_rows(o_ref, meta_ref, lambda rows: _mm(a_ref[rows, :], w_buf.at[slot]))


def _live_tile(i, meta):
    return jnp.minimum(i, meta[_T_N_TILES, 0] - 1)


def _expert_swiglu(xs, w_gate, w_up, tile_meta, tm):
    rows, d = xs.shape
    n = w_gate.shape[2]
    tn = _pick(n, 512, LANES)
    panel = pltpu.VMEM((2, d, tn), F32)
    return pl.pallas_call(
        _expert_swiglu_kernel,
        grid_spec=pltpu.PrefetchScalarGridSpec(
            num_scalar_prefetch=1,
            grid=(n // tn, rows // tm),
            in_specs=[pl.BlockSpec((tm, d), lambda j, i, meta: (_live_tile(i, meta), 0)),
                      pl.BlockSpec(memory_space=pl.ANY),
                      pl.BlockSpec(memory_space=pl.ANY)],
            out_specs=pl.BlockSpec((tm, tn), lambda j, i, meta: (i, j)),
            scratch_shapes=[panel, panel, pltpu.SemaphoreType.DMA((2,))],
        ),
        out_shape=jax.ShapeDtypeStruct((rows, n), BF16),
        compiler_params=_params("arbitrary", "arbitrary"),
        name="expert_swiglu",
    )(tile_meta, xs, w_gate, w_up)


def _expert_down(act, w_down, tile_meta, tm):
    rows, k = act.shape
    n = w_down.shape[2]
    tn = _pick(n, 1024, LANES)
    return pl.pallas_call(
        _expert_down_kernel,
        grid_spec=pltpu.PrefetchScalarGridSpec(
            num_scalar_prefetch=1,
            grid=(n // tn, rows // tm),
            in_specs=[pl.BlockSpec((tm, k), lambda j, i, meta: (_live_tile(i, meta), 0)),
                      pl.BlockSpec(memory_space=pl.ANY)],
            out_specs=pl.BlockSpec((tm, tn), lambda j, i, meta: (i, j)),
            scratch_shapes=[pltpu.VMEM((2, k, tn), F32), pltpu.SemaphoreType.DMA((2,))],
        ),
        out_shape=jax.ShapeDtypeStruct((rows, n), F32),
        compiler_params=_params("arbitrary", "arbitrary"),
        name="expert_down",
    )(tile_meta, act, w_down)


def _row_copy_in(y_ref, buf_ref, sems, slot, k, t, src_row):
    return pltpu.make_async_copy(y_ref.at[pl.ds(src_row, 1), :],
                                 buf_ref.at[slot, k, pl.ds(t, 1), :], sems.at[slot])


def _combine_kernel(pos_ref, x_ref, g_ref, info_ref, fg_ref, y_ref, o_ref, buf_ref, sems):
    tt = x_ref.shape[0]
    i = pl.program_id(0)
    slot = i % 2

    def start_gather(step, into):
        base = step * tt * TOP_K

        def issue(t, carry):
            for k in range(TOP_K):
                _row_copy_in(y_ref, buf_ref, sems, into, k, t,
                             pos_ref[base + t * TOP_K + k]).start()
            return carry

        lax.fori_loop(0, tt, issue, 0)

    @pl.when(i == 0)
    def _():
        start_gather(0, 0)

    @pl.when(i + 1 < pl.num_programs(0))
    def _():
        start_gather(i + 1, 1 - slot)

    def drain(t, carry):
        for k in range(TOP_K):
            _row_copy_in(y_ref, buf_ref, sems, slot, k, t, 0).wait()
        return carry

    lax.fori_loop(0, tt, drain, 0)
    info = info_ref[...]
    f = info[:, _W1:_W1 + 1] * buf_ref[slot, 0] + info[:, _W2:_W2 + 1] * buf_ref[slot, 1]
    xn = x_ref[...] + g_ref[...] * f
    o_ref[...] = xn * lax.rsqrt(jnp.mean(xn * xn, axis=-1, keepdims=True) + EPS) * fg_ref[...]


def _combine_norm(x2, gate, info, pos, y, final_g, seq):
    n, d = x2.shape
    tt = _pick(seq, 128, SUBLANES)
    tiles_per_batch = seq // tt
    return pl.pallas_call(
        _combine_kernel,
        grid_spec=pltpu.PrefetchScalarGridSpec(
            num_scalar_prefetch=1,
            grid=(n // tt,),
            in_specs=[pl.BlockSpec((tt, d), lambda i, pos: (i, 0)),
                      pl.BlockSpec((None, 1, d), lambda i, pos: (i // tiles_per_batch, 0, 0)),
                      pl.BlockSpec((tt, LANES), lambda i, pos: (i, 0)),
                      pl.BlockSpec((1, d), lambda i, pos: (0, 0)),
                      pl.BlockSpec(memory_space=pl.ANY)],
            out_specs=pl.BlockSpec((tt, d), lambda i, pos: (i, 0)),
            scratch_shapes=[pltpu.VMEM((2, TOP_K, tt, d), F32),
                            pltpu.SemaphoreType.DMA((2,))],
        ),
        out_shape=jax.ShapeDtypeStruct((n, d), F32),
        compiler_params=_params("arbitrary"),
        name="combine_norm",
    )(pos.reshape(-1), x2, gate, info, final_g.reshape(1, d), y)


def _moe_layer(x2, g, shift, scale, gate, w_router, w_gate, w_up, w_down, final_g, seq):
    n = x2.shape[0]
    n_exp = w_router.shape[1]
    h, logits = _modulate_router(x2, g, shift, scale, w_router, seq)
    info, cnt = _route(logits, n_exp)

    tm = _pick(n, 512, SUBLANES)
    max_tiles = n * TOP_K // tm + n_exp
    counts = cnt[0, :n_exp].astype(jnp.int32)
    tiles_per_exp = (counts + tm - 1) // tm
    tile_end = jnp.cumsum(tiles_per_exp)
    offsets = (tile_end - tiles_per_exp) * tm
    n_tiles = tile_end[-1:]
    tile_ids = jnp.arange(max_tiles, dtype=jnp.int32)
    tile_expert = jnp.minimum(
        jnp.sum((tile_ids[:, None] >= tile_end[None, :]).astype(jnp.int32), axis=1), n_exp - 1)
    live = tile_ids < n_tiles[0]
    first_tile = (tile_end - tiles_per_exp)[tile_expert]
    tile_rows = jnp.where(live, jnp.clip(counts[tile_expert] - (tile_ids - first_tile) * tm, 0, tm), 0)
    first = live & (tile_ids == first_tile)
    group = jnp.cumsum(first.astype(jnp.int32)) - 1
    has_tiles = tiles_per_exp > 0
    experts = jnp.arange(n_exp, dtype=jnp.int32)
    later = (experts[:, None] + 1 + experts[None, :]) % n_exp
    steps_ahead = jnp.min(jnp.where(has_tiles[later], experts[None, :], n_exp), axis=1)
    next_expert = jnp.take_along_axis(later, jnp.minimum(steps_ahead, n_exp - 1)[:, None], axis=1)[:, 0]
    last_expert = jnp.max(jnp.where(has_tiles, experts, 0))
    tile_meta = jnp.stack([
        tile_expert,
        tile_rows,
        first.astype(jnp.int32),
        group,
        next_expert[tile_expert],
        (tile_expert == last_expert).astype(jnp.int32),
        jnp.broadcast_to(n_tiles, (max_tiles,)),
        jnp.broadcast_to(jnp.sum(first.astype(jnp.int32)), (max_tiles,)),
    ]).astype(jnp.int32)
    idx = info[:, _I1:_I2 + 1].astype(jnp.int32)
    pos = offsets[idx] + info[:, _R1:_R2 + 1].astype(jnp.int32)

    xs = _dispatch(h, pos, max_tiles * tm)
    act = _expert_swiglu(xs, w_gate, w_up, tile_meta, tm)
    y = _expert_down(act, w_down, tile_meta, tm)
    return _combine_norm(x2, gate, info, pos, y, final_g, seq)


def _rope_tables(seq):
    n_freq = HEAD_DIM // 4
    t = jnp.arange(seq, dtype=jnp.int32)
    inv_freq = 1.0 / (ROPE_THETA ** (jnp.arange(n_freq, dtype=F32) * 2.0 / (HEAD_DIM // 2)))
    row = (t // GRID_W).astype(F32)[:, None] * inv_freq
    col = (t % GRID_W).astype(F32)[:, None] * inv_freq
    cos = jnp.concatenate([jnp.cos(row), jnp.cos(row), jnp.cos(col), jnp.cos(col)], axis=1)
    sin = jnp.concatenate([-jnp.sin(row), jnp.sin(row), -jnp.sin(col), jnp.sin(col)], axis=1)
    return cos, sin


def kernel(x, c, w_mod, b_mod, norm_g, final_g, attn_w_qkv, attn_q_gain, attn_k_gain, attn_w_o, mix_w_uv, mix_v_gain, mix_w_s, mix_b_s, mix_w_out, ffn_w_gate, ffn_w_up, ffn_w_down, moe_w_router, moe_w_gate, moe_w_up, moe_w_down):
    batch, seq, d = x.shape
    depth = w_mod.shape[0]
    assert depth == 2 and d % (GQA_GROUP * HEAD_DIM) == 0 and seq % CHUNK == 0
    n_q = d // HEAD_DIM
    n_kv = n_q // GQA_GROUP
    cos, sin = _rope_tables(seq)

    cond = _cond_lanes(c)
    mod0 = _adaln_mod(cond, w_mod, b_mod, 1)
    mods = [mod0[0].reshape(batch, 6, 1, d), None]
    vec = lambda layer, which: mods[layer][:, which]
    x2 = x.reshape(batch * seq, d)

    h = _modulate(x2, norm_g[0, 0], vec(0, 0), vec(0, 1), seq)
    qkv, mod1 = _qkv_proj(h, attn_w_qkv[0], attn_q_gain[0], attn_k_gain[0], cos, sin, seq, n_q,
                          n_kv, cond, w_mod, b_mod, 1)
    mods[1] = mod1.reshape(batch, 6, 1, d)
    o = _attention(qkv, batch, seq, n_q, n_kv)
    x2 = _residual_proj(o, attn_w_o[0], x2, vec(0, 2), seq, 512, 1024, False)
    h = _modulate(x2, norm_g[0, 1], vec(0, 3), vec(0, 4), seq)
    act = _swiglu_proj(h, ffn_w_gate[0], ffn_w_up[0])
    x2 = _residual_proj(act, ffn_w_down[0], x2, vec(0, 5), seq, 512, 512, True)

    h = _modulate(x2, norm_g[1, 0], vec(1, 0), vec(1, 1), seq)
    z = _gelu_proj(h, mix_w_uv[0])
    gated = _spatial_gate(z, mix_v_gain[0], mix_w_s[0], mix_b_s[0])
    x2 = _residual_proj(gated, mix_w_out[0], x2, vec(1, 2), seq, 512, 1024, False)
    out = _moe_layer(x2, norm_g[1, 1], vec(1, 3), vec(1, 4), vec(1, 5), moe_w_router[0],
                     moe_w_gate[0], moe_w_up[0], moe_w_down[0], final_g, seq)
    return out.reshape(batch, seq, d)
```

```python
import functools

import jax
import jax.numpy as jnp
from jax import lax
from jax.experimental import pallas as pl
from jax.experimental.pallas import tpu as pltpu

EPS = 1e-6
HEAD_DIM = 128
GQA_GROUP = 4
GRID_W = 64
ROPE_THETA = 10000.0
CHUNK = 128
TOP_K = 2
LOG2_E = 1.4426950408889634
LANES = 128
SUBLANES = 8
QKV_ROW_SPLITS = 2
VMEM_LIMIT_BYTES = 56 * 1024 * 1024

F32 = jnp.float32
BF16 = jnp.bfloat16


def _pick(dim, pref, mult):
    t = max(min(pref, dim) // mult * mult, mult)
    while t > mult and dim % t:
        t -= mult
    assert dim % t == 0, (dim, pref, mult)
    return t


def _params(*sem):
    return pltpu.CompilerParams(dimension_semantics=sem, vmem_limit_bytes=VMEM_LIMIT_BYTES)


def _silu(v):
    return v * jax.nn.sigmoid(v)


def _mod_kernel(c_ref, w_ref, b_ref, o_ref, acc_ref):
    k = pl.program_id(2)
    nb = c_ref.shape[0]
    tk, tn = w_ref.shape

    @pl.when(k == 0)
    def _():
        acc_ref[...] = jnp.zeros_like(acc_ref)

    for b in range(nb):
        cb = _silu(c_ref[b])
        parts = []
        for j in range(tn // LANES):
            prod = w_ref[:, j * LANES:(j + 1) * LANES] * cb
            parts.append(prod.reshape(tk // SUBLANES, SUBLANES, LANES).sum(axis=0))
        acc_ref[b] += jnp.concatenate(parts, axis=1)

    @pl.when(k == pl.num_programs(2) - 1)
    def _():
        for b in range(nb):
            o_ref[b:b + 1, :] = acc_ref[b].sum(axis=0, keepdims=True) + b_ref[...]


def _adaln_mod(c, w_mod, b_mod):
    depth, d, n = w_mod.shape
    nb = c.shape[0]
    tk = _pick(d, 512, SUBLANES)
    tn = _pick(n, 2048, LANES)
    c_rep = jnp.broadcast_to(c[:, :, None], (nb, d, LANES))
    return pl.pallas_call(
        _mod_kernel,
        grid=(depth, n // tn, d // tk),
        in_specs=[
            pl.BlockSpec((nb, tk, LANES), lambda l, j, k: (0, k, 0)),
            pl.BlockSpec((None, tk, tn), lambda l, j, k: (l, k, j)),
            pl.BlockSpec((None, 1, tn), lambda l, j, k: (l, 0, j)),
        ],
        out_specs=pl.BlockSpec((None, nb, tn), lambda l, j, k: (l, 0, j)),
        out_shape=jax.ShapeDtypeStruct((depth, nb, n), F32),
        scratch_shapes=[pltpu.VMEM((nb, SUBLANES, tn), F32)],
        compiler_params=_params("arbitrary", "arbitrary", "arbitrary"),
        name="adaln_mod",
    )(c_rep, w_mod, b_mod.reshape(depth, 1, n))


def _modulated(x_ref, g_ref, sh_ref, sc_ref):
    x = x_ref[...]
    y = x * lax.rsqrt(jnp.mean(x * x, axis=-1, keepdims=True) + EPS) * g_ref[...]
    return y * (1.0 + sc_ref[...]) + sh_ref[...]


def _modulate_kernel(x_ref, g_ref, sh_ref, sc_ref, o_ref):
    o_ref[...] = _modulated(x_ref, g_ref, sh_ref, sc_ref).astype(o_ref.dtype)


def _modulate_router_kernel(x_ref, g_ref, sh_ref, sc_ref, whi_ref, wlo_ref, o_ref, lg_ref):
    h = _modulated(x_ref, g_ref, sh_ref, sc_ref)
    hb = h.astype(BF16)
    o_ref[...] = hb
    h_lo = (h - hb.astype(F32)).astype(BF16)
    lg = jnp.dot(hb, whi_ref[...], preferred_element_type=F32)
    lg += jnp.dot(h_lo, whi_ref[...], preferred_element_type=F32)
    lg += jnp.dot(hb, wlo_ref[...], preferred_element_type=F32)
    lg_ref[...] = lg


def _row_specs(d, tm, tiles_per_batch):
    vec = pl.BlockSpec((None, 1, d), lambda i: (i // tiles_per_batch, 0, 0))
    return [
        pl.BlockSpec((tm, d), lambda i: (i, 0)),
        pl.BlockSpec((1, d), lambda i: (0, 0)),
        vec,
        vec,
    ]


def _modulate(x2, g, shift, scale, seq):
    m, d = x2.shape
    tm = _pick(seq, 256, SUBLANES)
    return pl.pallas_call(
        _modulate_kernel,
        grid=(m // tm,),
        in_specs=_row_specs(d, tm, seq // tm),
        out_specs=pl.BlockSpec((tm, d), lambda i: (i, 0)),
        out_shape=jax.ShapeDtypeStruct((m, d), BF16),
        compiler_params=_params("parallel"),
        name="modulate",
    )(x2, g.reshape(1, d), shift, scale)


def _modulate_router(x2, g, shift, scale, w_router, seq):
    m, d = x2.shape
    n_exp = w_router.shape[1]
    tm = _pick(seq, 256, SUBLANES)
    w_pad = jnp.zeros((d, LANES), F32).at[:, :n_exp].set(w_router)
    w_hi = w_pad.astype(BF16)
    w_lo = (w_pad - w_hi.astype(F32)).astype(BF16)
    wspec = pl.BlockSpec((d, LANES), lambda i: (0, 0))
    return pl.pallas_call(
        _modulate_router_kernel,
        grid=(m // tm,),
        in_specs=_row_specs(d, tm, seq // tm) + [wspec, wspec],
        out_specs=[pl.BlockSpec((tm, d), lambda i: (i, 0)),
                   pl.BlockSpec((tm, LANES), lambda i: (i, 0))],
        out_shape=[jax.ShapeDtypeStruct((m, d), BF16),
                   jax.ShapeDtypeStruct((m, LANES), F32)],
        compiler_params=_params("parallel"),
        name="modulate_router",
    )(x2, g.reshape(1, d), shift, scale, w_hi, w_lo)


def _mm(a, w_ref):
    return jnp.dot(a, w_ref[...].astype(BF16), preferred_element_type=F32)


def _rope_rotate(v, first_half):
    return jnp.where(first_half, pltpu.roll(v, LANES - 32, 1), pltpu.roll(v, 32, 1))


def _qkv_kernel(a_ref, w_ref, gain_ref, cos_ref, sin_ref, o_ref):
    half = a_ref.shape[0] // QKV_ROW_SPLITS
    lane = lax.broadcasted_iota(jnp.int32, (half, HEAD_DIM), 1)
    first_half = (lane % 64) < 32
    for r in range(QKV_ROW_SPLITS):
        rows = slice(r * half, (r + 1) * half)
        acc = _mm(a_ref[rows, :], w_ref)
        cos = cos_ref[rows, :]
        sin = sin_ref[rows, :]
        for h in range(acc.shape[1] // HEAD_DIM):
            sl = slice(h * HEAD_DIM, (h + 1) * HEAD_DIM)
            v = acc[:, sl]
            norm = lax.rsqrt(jnp.mean(v * v, axis=-1, keepdims=True) + EPS) * gain_ref[0:1, sl]
            v = v * jnp.where(gain_ref[1:2, sl] > 0.0, norm, 1.0)
            o_ref[rows, sl] = (v * cos + _rope_rotate(v, first_half) * sin).astype(o_ref.dtype)


def _qkv_proj(h, w_qkv, q_gain, k_gain, cos, sin, seq, n_q, n_kv):
    m, d = h.shape
    n = w_qkv.shape[1]
    tm = _pick(seq, 512, SUBLANES)
    tn = _pick(n_kv * HEAD_DIM, 1024, LANES)
    heads_per_tile = tn // HEAD_DIM
    n_q_tiles = n_q * HEAD_DIM // tn
    n_k_tiles = n_kv * HEAD_DIM // tn
    n_rope_tiles = n_q_tiles + n_k_tiles
    qg = jnp.tile(q_gain * (HEAD_DIM ** -0.5 * LOG2_E), heads_per_tile)
    kg = jnp.tile(k_gain, heads_per_tile)
    gains = jnp.concatenate([jnp.broadcast_to(qg, (n_q_tiles, tn)),
                             jnp.broadcast_to(kg, (n_k_tiles, tn)),
                             jnp.ones((n_k_tiles, tn), F32)])
    normed = jnp.broadcast_to((jnp.arange(n // tn) < n_rope_tiles).astype(F32)[:, None],
                              (n // tn, tn))
    gains = jnp.stack([gains, normed], axis=1)
    cos2 = jnp.stack([cos, jnp.ones_like(cos)])
    sin2 = jnp.stack([sin, jnp.zeros_like(sin)])
    tiles_per_seq = seq // tm
    tab = pl.BlockSpec((None, tm, HEAD_DIM),
                       lambda j, i: ((j >= n_rope_tiles).astype(jnp.int32), i % tiles_per_seq, 0))
    return pl.pallas_call(
        _qkv_kernel,
        grid=(n // tn, m // tm),
        in_specs=[
            pl.BlockSpec((tm, d), lambda j, i: (i, 0)),
            pl.BlockSpec((d, tn), lambda j, i: (0, j)),
            pl.BlockSpec((None, 2, tn), lambda j, i: (j, 0, 0)),
            tab,
            tab,
        ],
        out_specs=pl.BlockSpec((tm, tn), lambda j, i: (i, j)),
        out_shape=jax.ShapeDtypeStruct((m, n), BF16),
        compiler_params=_params("arbitrary", "arbitrary"),
        name="qkv_proj",
    )(h, w_qkv, gains, cos2, sin2)


def _gelu_kernel(a_ref, w_ref, o_ref):
    acc = _mm(a_ref[...], w_ref)
    o_ref[...] = (0.5 * acc * (1.0 + lax.erf(acc * (2.0 ** -0.5)))).astype(o_ref.dtype)


def _gelu_proj(h, w):
    m, d = h.shape
    n = w.shape[1]
    tm = _pick(m, 512, SUBLANES)
    tn = _pick(n, 1024, LANES)
    return pl.pallas_call(
        _gelu_kernel,
        grid=(n // tn, m // tm),
        in_specs=[pl.BlockSpec((tm, d), lambda j, i: (i, 0)),
                  pl.BlockSpec((d, tn), lambda j, i: (0, j))],
        out_specs=pl.BlockSpec((tm, tn), lambda j, i: (i, j)),
        out_shape=jax.ShapeDtypeStruct((m, n), BF16),
        compiler_params=_params("arbitrary", "arbitrary"),
        name="gelu_proj",
    )(h, w)


def _residual_kernel(a_ref, w_ref, x_ref, g_ref, o_ref):
    o_ref[...] = x_ref[...] + g_ref[...] * _mm(a_ref[...], w_ref)


def _residual_proj(a, w, x2, gate, seq, tm_pref, tn_pref, single_buffer_w):
    m, k = a.shape
    n = w.shape[1]
    tm = _pick(seq, tm_pref, SUBLANES)
    tn = _pick(n, tn_pref, LANES)
    tiles_per_batch = seq // tm
    w_mode = dict(pipeline_mode=pl.Buffered(1)) if single_buffer_w else {}
    return pl.pallas_call(
        _residual_kernel,
        grid=(n // tn, m // tm),
        in_specs=[
            pl.BlockSpec((tm, k), lambda j, i: (i, 0)),
            pl.BlockSpec((k, tn), lambda j, i: (0, j), **w_mode),
            pl.BlockSpec((tm, tn), lambda j, i: (i, j)),
            pl.BlockSpec((None, 1, tn), lambda j, i: (i // tiles_per_batch, 0, j)),
        ],
        out_specs=pl.BlockSpec((tm, tn), lambda j, i: (i, j)),
        out_shape=jax.ShapeDtypeStruct((m, n), F32),
        compiler_params=_params("arbitrary", "arbitrary"),
        name="residual_proj",
    )(a, w, x2, gate)


def _swiglu_kernel(a_ref, wg_ref, wu_ref, o_ref):
    a = a_ref[...]
    o_ref[...] = (_silu(_mm(a, wg_ref)) * _mm(a, wu_ref)).astype(o_ref.dtype)


def _swiglu_proj(h, w_gate, w_up):
    m, d = h.shape
    n = w_gate.shape[1]
    tm = _pick(m, 2048, SUBLANES)
    tn = _pick(n, 256, LANES)
    wspec = pl.BlockSpec((d, tn), lambda j, i: (0, j))
    return pl.pallas_call(
        _swiglu_kernel,
        grid=(n // tn, m // tm),
        in_specs=[pl.BlockSpec((tm, d), lambda j, i: (i, 0)), wspec, wspec],
        out_specs=pl.BlockSpec((tm, tn), lambda j, i: (i, j)),
        out_shape=jax.ShapeDtypeStruct((m, n), BF16),
        compiler_params=_params("arbitrary", "arbitrary"),
        name="swiglu_proj",
    )(h, w_gate, w_up)


def _attn_kernel(q_ref, k_ref, v_ref, o_ref, m_ref, l_ref, acc_ref, *, tk):
    tq = q_ref.shape[0]
    seq = k_ref.shape[0]
    kv_heads = k_ref.shape[1] // HEAD_DIM
    lane_tiles = tk // LANES
    group_w = GQA_GROUP * HEAD_DIM
    qs = [jnp.concatenate([q_ref[:, j * group_w + g * HEAD_DIM:j * group_w + (g + 1) * HEAD_DIM]
                           for g in range(GQA_GROUP)], axis=0) for j in range(kv_heads)]
    m_ref[...] = jnp.full_like(m_ref, -jnp.inf)
    l_ref[...] = jnp.zeros_like(l_ref)
    acc_ref[...] = jnp.zeros_like(acc_ref)

    def body(c, carry):
        start = pl.multiple_of(c * tk, tk)
        for j in range(kv_heads):
            cols = slice(j * HEAD_DIM, (j + 1) * HEAD_DIM)
            k = k_ref[pl.ds(start, tk), cols]
            v = v_ref[pl.ds(start, tk), cols]
            s = lax.dot_general(qs[j], k, (((1,), (1,)), ((), ())), preferred_element_type=F32)
            tiles = [s[:, t * LANES:(t + 1) * LANES] for t in range(lane_tiles)]
            tile_max = functools.reduce(jnp.maximum, tiles)
            m_prev = m_ref[j]
            m_new = jnp.maximum(m_prev, jnp.max(tile_max, axis=-1, keepdims=True))
            alpha = jnp.exp2(m_prev - m_new)
            p = [jnp.exp2(t - m_new) for t in tiles]
            l_ref[j] = alpha * l_ref[j] + functools.reduce(jnp.add, p)
            pb = jnp.concatenate([t.astype(BF16) for t in p], axis=1)
            acc_ref[j] = alpha * acc_ref[j] + jnp.dot(pb, v, preferred_element_type=F32)
            m_ref[j] = m_new
        return carry

    lax.fori_loop(0, seq // tk, body, 0, unroll=True)
    for j in range(kv_heads):
        out = acc_ref[j] / jnp.sum(l_ref[j], axis=-1, keepdims=True)
        for g in range(GQA_GROUP):
            cols = slice(j * group_w + g * HEAD_DIM, j * group_w + (g + 1) * HEAD_DIM)
            o_ref[:, cols] = out[g * tq:(g + 1) * tq].astype(o_ref.dtype)


def _attention(qkv, batch, seq, n_q, n_kv):
    m = qkv.shape[0]
    tq = _pick(seq, 256, 16)
    tk = _pick(seq, 256, LANES)
    kv_heads = 2 if n_kv % 2 == 0 else 1
    group_w = kv_heads * GQA_GROUP * HEAD_DIM
    kv_w = kv_heads * HEAD_DIM
    q_tiles = seq // tq
    k_block0 = n_q * HEAD_DIM // kv_w
    v_block0 = (n_q + n_kv) * HEAD_DIM // kv_w
    state = pltpu.VMEM((kv_heads, GQA_GROUP * tq, LANES), F32)
    return pl.pallas_call(
        functools.partial(_attn_kernel, tk=tk),
        grid=(batch, n_kv // kv_heads, q_tiles),
        in_specs=[
            pl.BlockSpec((tq, group_w), lambda b, j, i: (b * q_tiles + i, j)),
            pl.BlockSpec((seq, kv_w), lambda b, j, i: (b, k_block0 + j)),
            pl.BlockSpec((seq, kv_w), lambda b, j, i: (b, v_block0 + j)),
        ],
        out_specs=pl.BlockSpec((tq, group_w), lambda b, j, i: (b * q_tiles + i, j)),
        out_shape=jax.ShapeDtypeStruct((m, n_q * HEAD_DIM), BF16),
        scratch_shapes=[state, state, state],
        compiler_params=_params("parallel", "parallel", "arbitrary"),
        name="attention",
    )(qkv, qkv, qkv)


def _sgu_kernel(u_ref, v_ref, gain_ref, ws_ref, bias_ref, o_ref, wsb_ref):
    @pl.when(pl.program_id(0) == 0)
    def _():
        wsb_ref[...] = ws_ref[...].astype(BF16)

    rows = u_ref.shape[0]
    groups = ws_ref.shape[0]
    for c in range(rows // CHUNK):
        rs = slice(c * CHUNK, (c + 1) * CHUNK)
        v = v_ref[rs, :].astype(F32)
        vn = v * lax.rsqrt(jnp.mean(v * v, axis=-1, keepdims=True) + EPS) * gain_ref[...]
        vn = vn.astype(BF16)
        for g in range(groups):
            cs = slice(g * LANES, (g + 1) * LANES)
            sv = jnp.dot(wsb_ref[g], vn[:, cs], preferred_element_type=F32) + bias_ref[g]
            o_ref[rs, cs] = (u_ref[rs, cs].astype(F32) * sv).astype(o_ref.dtype)


def _spatial_gate(z, v_gain, w_s, b_s):
    m, two_w = z.shape
    width = two_w // 2
    groups = w_s.shape[0]
    rows = _pick(m, 2 * CHUNK, CHUNK)
    bias = jnp.broadcast_to(b_s[:, :, None], (groups, CHUNK, LANES))
    full = lambda shape: pl.BlockSpec(shape, lambda i: (0,) * len(shape))
    return pl.pallas_call(
        _sgu_kernel,
        grid=(m // rows,),
        in_specs=[
            pl.BlockSpec((rows, width), lambda i: (i, 0)),
            pl.BlockSpec((rows, width), lambda i: (i, 1)),
            full((1, width)),
            full((groups, CHUNK, CHUNK)),
            full((groups, CHUNK, LANES)),
        ],
        out_specs=pl.BlockSpec((rows, width), lambda i: (i, 0)),
        out_shape=jax.ShapeDtypeStruct((m, width), BF16),
        scratch_shapes=[pltpu.VMEM((groups, CHUNK, CHUNK), BF16)],
        compiler_params=_params("arbitrary"),
        name="spatial_gate",
    )(z, z, v_gain.reshape(1, width), w_s, bias)


_I1, _I2, _W1, _W2, _R1, _R2 = range(6)


def _route_kernel(lg_ref, info_ref, cnt_ref, carry_ref, *, n_exp):
    @pl.when(pl.program_id(0) == 0)
    def _():
        carry_ref[...] = jnp.zeros_like(carry_ref)

    lg = lg_ref[...]
    tb = lg.shape[0]
    lane = lax.broadcasted_iota(jnp.int32, lg.shape, 1).astype(F32)
    lg = jnp.where(lane < n_exp, lg, -jnp.inf)
    v1 = jnp.max(lg, axis=-1, keepdims=True)
    i1 = jnp.min(jnp.where(lg == v1, lane, float(LANES)), axis=-1, keepdims=True)
    rest = jnp.where(lane == i1, -jnp.inf, lg)
    v2 = jnp.max(rest, axis=-1, keepdims=True)
    i2 = jnp.min(jnp.where(rest == v2, lane, float(LANES)), axis=-1, keepdims=True)
    e = jnp.exp(v2 - v1)
    w1 = 1.0 / (1.0 + e)
    w2 = e / (1.0 + e)
    sel = jnp.where(lane == i1, 1.0, jnp.where(lane == i2, 1.0, 0.0))
    row = lax.broadcasted_iota(jnp.int32, (tb, tb), 0)
    col = lax.broadcasted_iota(jnp.int32, (tb, tb), 1)
    tri = jnp.where(row > col, 1.0, 0.0).astype(BF16)
    rank = jnp.dot(tri, sel.astype(BF16), preferred_element_type=F32) + carry_ref[0:1, :]
    r1 = jnp.sum(jnp.where(lane == i1, rank, 0.0), axis=-1, keepdims=True)
    r2 = jnp.sum(jnp.where(lane == i2, rank, 0.0), axis=-1, keepdims=True)
    carry_ref[...] += jnp.sum(sel, axis=0, keepdims=True)
    info = jnp.zeros_like(lg)
    for idx, val in ((_I1, i1), (_I2, i2), (_W1, w1), (_W2, w2), (_R1, r1), (_R2, r2)):
        info = jnp.where(lane == idx, val, info)
    info_ref[...] = info
    cnt_ref[...] = carry_ref[...]


def _route(logits, n_exp):
    n = logits.shape[0]
    tb = _pick(n, 512, SUBLANES)
    return pl.pallas_call(
        functools.partial(_route_kernel, n_exp=n_exp),
        grid=(n // tb,),
        in_specs=[pl.BlockSpec((tb, LANES), lambda i: (i, 0))],
        out_specs=[pl.BlockSpec((tb, LANES), lambda i: (i, 0)),
                   pl.BlockSpec((SUBLANES, LANES), lambda i: (0, 0))],
        out_shape=[jax.ShapeDtypeStruct((n, LANES), F32),
                   jax.ShapeDtypeStruct((SUBLANES, LANES), F32)],
        scratch_shapes=[pltpu.VMEM((SUBLANES, LANES), F32)],
        compiler_params=_params("arbitrary"),
        name="route",
    )(logits)


def _row_copy_out(h_ref, xs_ref, sem, t, dst_row):
    return pltpu.make_async_copy(h_ref.at[t], xs_ref.at[dst_row], sem)


def _dispatch_kernel(pos_ref, h_ref, zero_ref, xs_ref, sem):
    del zero_ref
    tt = h_ref.shape[0]
    base = pl.program_id(0) * tt * TOP_K

    def issue(t, carry):
        for k in range(TOP_K):
            _row_copy_out(h_ref, xs_ref, sem, t, pos_ref[base + t * TOP_K + k]).start()
        return carry

    def drain(t, carry):
        for k in range(TOP_K):
            _row_copy_out(h_ref, xs_ref, sem, t, 0).wait()
        return carry

    lax.fori_loop(0, tt, issue, 0)
    lax.fori_loop(0, tt, drain, 0)


def _dispatch(h, pos, n_rows):
    n, d = h.shape
    sub = d // LANES
    tt = _pick(n, 256, SUBLANES)
    h3 = h.reshape(n, sub, LANES)
    zeros = jnp.zeros((n_rows, sub, LANES), h.dtype)
    xs = pl.pallas_call(
        _dispatch_kernel,
        grid_spec=pltpu.PrefetchScalarGridSpec(
            num_scalar_prefetch=1,
            grid=(n // tt,),
            in_specs=[pl.BlockSpec((tt, sub, LANES), lambda i, pos: (i, 0, 0)),
                      pl.BlockSpec(memory_space=pl.ANY)],
            out_specs=pl.BlockSpec(memory_space=pl.ANY),
            scratch_shapes=[pltpu.SemaphoreType.DMA(())],
        ),
        out_shape=jax.ShapeDtypeStruct((n_rows, sub, LANES), h.dtype),
        input_output_aliases={2: 0},
        compiler_params=_params("arbitrary"),
        name="dispatch",
    )(pos.reshape(-1), h3, zeros)
    return xs.reshape(n_rows, d)


(_T_EXPERT, _T_ROWS, _T_FIRST, _T_GROUP, _T_NEXT_EXPERT, _T_LAST_GROUP, _T_N_TILES,
 _T_N_GROUPS) = range(8)


def _per_tile_rows(o_ref, meta_ref, compute):
    tm = o_ref.shape[0]
    half = tm // 2
    valid = meta_ref[_T_ROWS, pl.program_id(1)]

    @pl.when(valid > half)
    def _():
        o_ref[...] = compute(slice(0, tm))

    @pl.when((valid > 0) & (valid <= half))
    def _():
        o_ref[0:half, :] = compute(slice(0, half))
        o_ref[half:tm, :] = jnp.zeros((tm - half, o_ref.shape[1]), o_ref.dtype)

    @pl.when(valid == 0)
    def _():
        o_ref[...] = jnp.zeros_like(o_ref)


def _panel_copies(w_hbm_refs, buf_refs, sems, expert, j, slot):
    tn = buf_refs[0].shape[2]
    cols = pl.ds(pl.multiple_of(j * tn, tn), tn)
    return [pltpu.make_async_copy(w.at[expert, :, cols], buf.at[slot], sems.at[slot])
            for w, buf in zip(w_hbm_refs, buf_refs)]


def _expert_panel_slot(meta_ref, w_hbm_refs, buf_refs, sems):
    j = pl.program_id(0)
    i = pl.program_id(1)
    slot = (j * meta_ref[_T_N_GROUPS, 0] + meta_ref[_T_GROUP, i]) % 2
    first = meta_ref[_T_FIRST, i] == 1

    @pl.when(first & (j == 0) & (i == 0))
    def _():
        for cp in _panel_copies(w_hbm_refs, buf_refs, sems, meta_ref[_T_EXPERT, i], j, slot):
            cp.start()

    @pl.when(first)
    def _():
        for cp in _panel_copies(w_hbm_refs, buf_refs, sems, meta_ref[_T_EXPERT, i], j, slot):
            cp.wait()
        j_next = j + meta_ref[_T_LAST_GROUP, i]

        @pl.when(j_next < pl.num_programs(0))
        def _():
            for cp in _panel_copies(w_hbm_refs, buf_refs, sems, meta_ref[_T_NEXT_EXPERT, i],
                                    j_next, 1 - slot):
                cp.start()

    return slot


def _expert_swiglu_kernel(meta_ref, a_ref, wg_hbm, wu_hbm, o_ref, wg_buf, wu_buf, sems):
    slot = _expert_panel_slot(meta_ref, (wg_hbm, wu_hbm), (wg_buf, wu_buf), sems)

    def compute(rows):
        a = a_ref[rows, :]
        return (_silu(_mm(a, wg_buf.at[slot])) * _mm(a, wu_buf.at[slot])).astype(o_ref.dtype)

    _per_tile_rows(o_ref, meta_ref, compute)


def _expert_down_kernel(meta_ref, a_ref, w_hbm, o_ref, w_buf, sems):
    slot = _expert_panel_slot(meta_ref, (w_hbm,), (w_buf,), sems)
    _per_tile_rows(o_ref, meta_ref, lambda rows: _mm(a_ref[rows, :], w_buf.at[slot]))


def _live_tile(i, meta):
    return jnp.minimum(i, meta[_T_N_TILES, 0] - 1)


def _expert_swiglu(xs, w_gate, w_up, tile_meta, tm):
    rows, d = xs.shape
    n = w_gate.shape[2]
    tn = _pick(n, 512, LANES)
    panel = pltpu.VMEM((2, d, tn), F32)
    return pl.pallas_call(
        _expert_swiglu_kernel,
        grid_spec=pltpu.PrefetchScalarGridSpec(
            num_scalar_prefetch=1,
            grid=(n // tn, rows // tm),
            in_specs=[pl.BlockSpec((tm, d), lambda j, i, meta: (_live_tile(i, meta), 0)),
                      pl.BlockSpec(memory_space=pl.ANY),
                      pl.BlockSpec(memory_space=pl.ANY)],
            out_specs=pl.BlockSpec((tm, tn), lambda j, i, meta: (i, j)),
            scratch_shapes=[panel, panel, pltpu.SemaphoreType.DMA((2,))],
        ),
        out_shape=jax.ShapeDtypeStruct((rows, n), BF16),
        compiler_params=_params("arbitrary", "arbitrary"),
        name="expert_swiglu",
    )(tile_meta, xs, w_gate, w_up)


def _expert_down(act, w_down, tile_meta, tm):
    rows, k = act.shape
    n = w_down.shape[2]
    tn = _pick(n, 1024, LANES)
    return pl.pallas_call(
        _expert_down_kernel,
        grid_spec=pltpu.PrefetchScalarGridSpec(
            num_scalar_prefetch=1,
            grid=(n // tn, rows // tm),
            in_specs=[pl.BlockSpec((tm, k), lambda j, i, meta: (_live_tile(i, meta), 0)),
                      pl.BlockSpec(memory_space=pl.ANY)],
            out_specs=pl.BlockSpec((tm, tn), lambda j, i, meta: (i, j)),
            scratch_shapes=[pltpu.VMEM((2, k, tn), F32), pltpu.SemaphoreType.DMA((2,))],
        ),
        out_shape=jax.ShapeDtypeStruct((rows, n), F32),
        compiler_params=_params("arbitrary", "arbitrary"),
        name="expert_down",
    )(tile_meta, act, w_down)


def _row_copy_in(y_ref, buf_ref, sems, slot, k, t, src_row):
    return pltpu.make_async_copy(y_ref.at[pl.ds(src_row, 1), :],
                                 buf_ref.at[slot, k, pl.ds(t, 1), :], sems.at[slot])


def _combine_kernel(pos_ref, x_ref, g_ref, info_ref, fg_ref, y_ref, o_ref, buf_ref, sems):
    tt = x_ref.shape[0]
    i = pl.program_id(0)
    slot = i % 2

    def start_gather(step, into):
        base = step * tt * TOP_K

        def issue(t, carry):
            for k in range(TOP_K):
                _row_copy_in(y_ref, buf_ref, sems, into, k, t,
                             pos_ref[base + t * TOP_K + k]).start()
            return carry

        lax.fori_loop(0, tt, issue, 0)

    @pl.when(i == 0)
    def _():
        start_gather(0, 0)

    @pl.when(i + 1 < pl.num_programs(0))
    def _():
        start_gather(i + 1, 1 - slot)

    def drain(t, carry):
        for k in range(TOP_K):
            _row_copy_in(y_ref, buf_ref, sems, slot, k, t, 0).wait()
        return carry

    lax.fori_loop(0, tt, drain, 0)
    info = info_ref[...]
    f = info[:, _W1:_W1 + 1] * buf_ref[slot, 0] + info[:, _W2:_W2 + 1] * buf_ref[slot, 1]
    xn = x_ref[...] + g_ref[...] * f
    o_ref[...] = xn * lax.rsqrt(jnp.mean(xn * xn, axis=-1, keepdims=True) + EPS) * fg_ref[...]


def _combine_norm(x2, gate, info, pos, y, final_g, seq):
    n, d = x2.shape
    tt = _pick(seq, 128, SUBLANES)
    tiles_per_batch = seq // tt
    return pl.pallas_call(
        _combine_kernel,
        grid_spec=pltpu.PrefetchScalarGridSpec(
            num_scalar_prefetch=1,
            grid=(n // tt,),
            in_specs=[pl.BlockSpec((tt, d), lambda i, pos: (i, 0)),
                      pl.BlockSpec((None, 1, d), lambda i, pos: (i // tiles_per_batch, 0, 0)),
                      pl.BlockSpec((tt, LANES), lambda i, pos: (i, 0)),
                      pl.BlockSpec((1, d), lambda i, pos: (0, 0)),
                      pl.BlockSpec(memory_space=pl.ANY)],
            out_specs=pl.BlockSpec((tt, d), lambda i, pos: (i, 0)),
            scratch_shapes=[pltpu.VMEM((2, TOP_K, tt, d), F32),
                            pltpu.SemaphoreType.DMA((2,))],
        ),
        out_shape=jax.ShapeDtypeStruct((n, d), F32),
        compiler_params=_params("arbitrary"),
        name="combine_norm",
    )(pos.reshape(-1), x2, gate, info, final_g.reshape(1, d), y)


def _moe_layer(x2, g, shift, scale, gate, w_router, w_gate, w_up, w_down, final_g, seq):
    n = x2.shape[0]
    n_exp = w_router.shape[1]
    h, logits = _modulate_router(x2, g, shift, scale, w_router, seq)
    info, cnt = _route(logits, n_exp)

    tm = _pick(n, 512, SUBLANES)
    max_tiles = n * TOP_K // tm + n_exp
    counts = cnt[0, :n_exp].astype(jnp.int32)
    tiles_per_exp = (counts + tm - 1) // tm
    tile_end = jnp.cumsum(tiles_per_exp)
    offsets = (tile_end - tiles_per_exp) * tm
    n_tiles = tile_end[-1:]
    tile_ids = jnp.arange(max_tiles, dtype=jnp.int32)
    tile_expert = jnp.minimum(
        jnp.sum((tile_ids[:, None] >= tile_end[None, :]).astype(jnp.int32), axis=1), n_exp - 1)
    live = tile_ids < n_tiles[0]
    first_tile = (tile_end - tiles_per_exp)[tile_expert]
    tile_rows = jnp.where(live, jnp.clip(counts[tile_expert] - (tile_ids - first_tile) * tm, 0, tm), 0)
    first = live & (tile_ids == first_tile)
    group = jnp.cumsum(first.astype(jnp.int32)) - 1
    has_tiles = tiles_per_exp > 0
    experts = jnp.arange(n_exp, dtype=jnp.int32)
    later = (experts[:, None] + 1 + experts[None, :]) % n_exp
    steps_ahead = jnp.min(jnp.where(has_tiles[later], experts[None, :], n_exp), axis=1)
    next_expert = jnp.take_along_axis(later, jnp.minimum(steps_ahead, n_exp - 1)[:, None], axis=1)[:, 0]
    last_expert = jnp.max(jnp.where(has_tiles, experts, 0))
    tile_meta = jnp.stack([
        tile_expert,
        tile_rows,
        first.astype(jnp.int32),
        group,
        next_expert[tile_expert],
        (tile_expert == last_expert).astype(jnp.int32),
        jnp.broadcast_to(n_tiles, (max_tiles,)),
        jnp.broadcast_to(jnp.sum(first.astype(jnp.int32)), (max_tiles,)),
    ]).astype(jnp.int32)
    idx = info[:, _I1:_I2 + 1].astype(jnp.int32)
    pos = offsets[idx] + info[:, _R1:_R2 + 1].astype(jnp.int32)

    xs = _dispatch(h, pos, max_tiles * tm)
    act = _expert_swiglu(xs, w_gate, w_up, tile_meta, tm)
    y = _expert_down(act, w_down, tile_meta, tm)
    return _combine_norm(x2, gate, info, pos, y, final_g, seq)


def _rope_tables(seq):
    n_freq = HEAD_DIM // 4
    t = jnp.arange(seq, dtype=jnp.int32)
    inv_freq = 1.0 / (ROPE_THETA ** (jnp.arange(n_freq, dtype=F32) * 2.0 / (HEAD_DIM // 2)))
    row = (t // GRID_W).astype(F32)[:, None] * inv_freq
    col = (t % GRID_W).astype(F32)[:, None] * inv_freq
    cos = jnp.concatenate([jnp.cos(row), jnp.cos(row), jnp.cos(col), jnp.cos(col)], axis=1)
    sin = jnp.concatenate([-jnp.sin(row), jnp.sin(row), -jnp.sin(col), jnp.sin(col)], axis=1)
    return cos, sin


def kernel(x, c, w_mod, b_mod, norm_g, final_g, attn_w_qkv, attn_q_gain, attn_k_gain, attn_w_o, mix_w_uv, mix_v_gain, mix_w_s, mix_b_s, mix_w_out, ffn_w_gate, ffn_w_up, ffn_w_down, moe_w_router, moe_w_gate, moe_w_up, moe_w_down):
    batch, seq, d = x.shape
    depth = w_mod.shape[0]
    assert depth == 2 and d % (GQA_GROUP * HEAD_DIM) == 0 and seq % CHUNK == 0
    n_q = d // HEAD_DIM
    n_kv = n_q // GQA_GROUP
    cos, sin = _rope_tables(seq)

    mod = _adaln_mod(c, w_mod, b_mod).reshape(depth, batch, 6, 1, d)
    vec = lambda layer, which: mod[layer, :, which]
    x2 = x.reshape(batch * seq, d)

    h = _modulate(x2, norm_g[0, 0], vec(0, 0), vec(0, 1), seq)
    qkv = _qkv_proj(h, attn_w_qkv[0], attn_q_gain[0], attn_k_gain[0], cos, sin, seq, n_q, n_kv)
    o = _attention(qkv, batch, seq, n_q, n_kv)
    x2 = _residual_proj(o, attn_w_o[0], x2, vec(0, 2), seq, 512, 1024, False)
    h = _modulate(x2, norm_g[0, 1], vec(0, 3), vec(0, 4), seq)
    act = _swiglu_proj(h, ffn_w_gate[0], ffn_w_up[0])
    x2 = _residual_proj(act, ffn_w_down[0], x2, vec(0, 5), seq, 512, 512, True)

    h = _modulate(x2, norm_g[1, 0], vec(1, 0), vec(1, 1), seq)
    z = _gelu_proj(h, mix_w_uv[0])
    gated = _spatial_gate(z, mix_v_gain[0], mix_w_s[0], mix_b_s[0])
    x2 = _residual_proj(gated, mix_w_out[0], x2, vec(1, 2), seq, 512, 1024, False)
    out = _moe_layer(x2, norm_g[1, 1], vec(1, 3), vec(1, 4), vec(1, 5), moe_w_router[0],
                     moe_w_gate[0], moe_w_up[0], moe_w_down[0], final_g, seq)
    return out.reshape(batch, seq, d)
```
